```python
import math
import jax
import jax.numpy as jnp
from jax import lax
import numpy as np

D_MODEL = 1024
BATCH = 16
SEQ = 2048
DEPTH = 1

CTX_LEN = 256
GRID_W = 64
N_HEADS = 8
HEAD_DIM = 64
V_DIM = 2 * HEAD_DIM
QK_WIDTH = N_HEADS * 2 * HEAD_DIM
ATTN_WIDTH = N_HEADS * V_DIM
POOL_WINDOWS = (2, 4, 8, 16)
POOL_GROUPS = len(POOL_WINDOWS)
POOL_GROUP_DIM = 128
POOL_WIDTH = POOL_GROUPS * POOL_GROUP_DIM
N_BRANCH = 2
D_FF = 2816
ROPE_BASE = 10000.0
ROPE_AXIS_DIM = HEAD_DIM // 2
Q_BLOCK = 128
N_MOD = 9
EPS = 1e-6

Q_OFF = 0
K_OFF = Q_OFF + QK_WIDTH
V_OFF = K_OFF + QK_WIDTH
P_OFF = V_OFF + ATTN_WIDTH
G_OFF = P_OFF + POOL_WIDTH
IN_COLS = G_OFF + N_BRANCH * D_MODEL

kernel_name = "hybrid_diffattn_pool_macaron_dit"


def rms_norm(x, g):
    xf = x.astype(jnp.float32)
    y = xf * lax.rsqrt(jnp.mean(xf * xf, axis=-1, keepdims=True) + EPS)
    return (y * g.astype(jnp.float32)).astype(x.dtype)


def modulate(h, shift, scale):
    return h * (1 + scale) + shift


def swiglu(h, w_gu, w_down):
    a, b = jnp.split(h @ w_gu, 2, axis=-1)
    return (jax.nn.silu(a) * b) @ w_down


def rope_1d(x, pos):
    half = ROPE_AXIS_DIM // 2
    freqs = ROPE_BASE ** (-jnp.arange(half, dtype=jnp.float32) / half)
    ang = pos.astype(jnp.float32)[:, None] * freqs[None, :]
    cos = jnp.cos(ang)[None, :, None, None, :].astype(x.dtype)
    sin = jnp.sin(ang)[None, :, None, None, :].astype(x.dtype)
    x1, x2 = x[..., :half], x[..., half:]
    return jnp.concatenate([x1 * cos - x2 * sin, x2 * cos + x1 * sin], axis=-1)


def rope_2d(x, row_ids, col_ids):
    return jnp.concatenate([rope_1d(x[..., :ROPE_AXIS_DIM], row_ids),
                            rope_1d(x[..., ROPE_AXIS_DIM:], col_ids)], axis=-1)


def split_qk(z):
    return z.reshape(z.shape[0], z.shape[1], N_HEADS, 2, HEAD_DIM)


def split_v(z):
    return z.reshape(z.shape[0], z.shape[1], N_HEADS, V_DIM)


def diff_attn_block(q, k, v, lam):
    s = jnp.einsum("bqhcd,bkhcd->bhcqk", q, k).astype(jnp.float32) * (HEAD_DIM ** -0.5)
    p = jax.nn.softmax(s, axis=-1)
    p = p[:, :, 0] - lam * p[:, :, 1]
    return jnp.einsum("bhqk,bkhe->bqhe", p.astype(v.dtype), v)


def diff_attn_post(o, g_subln, lam_init):
    b, l = o.shape[0], o.shape[1]
    return (rms_norm(o, g_subln) * (1.0 - lam_init)).reshape(b, l, ATTN_WIDTH)


def pool_mix(u, w_pool, pool_scale):
    b, l, _ = u.shape
    t = jnp.arange(l)
    outs = []
    for g, w in enumerate(POOL_WINDOWS):
        ug = u[..., g * POOL_GROUP_DIM:(g + 1) * POOL_GROUP_DIM].astype(jnp.float32)
        cs = jnp.concatenate([jnp.zeros((b, 1, POOL_GROUP_DIM), jnp.float32),
                              jnp.cumsum(ug, axis=1)], axis=1)
        lo = jnp.clip(t - w // 2, 0, l)
        hi = jnp.clip(t + w - w // 2, 0, l)
        mean = (cs[:, hi] - cs[:, lo]) / (hi - lo).astype(jnp.float32)[None, :, None]
        pooled = (mean - ug).astype(u.dtype)
        outs.append(pooled @ w_pool[g])
    return jnp.concatenate(outs, axis=-1) * pool_scale


def branch_merge(attn, pool, gates, w_branch_attn, w_branch_pool, w_out):
    g_attn, g_pool = jnp.split(gates, N_BRANCH, axis=-1)
    y = jax.nn.sigmoid(g_attn) * (attn @ w_branch_attn) + jax.nn.sigmoid(g_pool) * (pool @ w_branch_pool)
    return y @ w_out


def setup_inputs(seed: int = 0) -> dict:
    key = jax.random.key(seed)
    ks = jax.random.split(key, 21)

    def nrm(k, shape, s):
        return jax.random.normal(k, shape, jnp.float32) * s

    return {
        "x": nrm(ks[0], (BATCH, SEQ, D_MODEL), 1.0),
        "c": nrm(ks[1], (BATCH, D_MODEL), 1.0),
        "ctx": nrm(ks[2], (BATCH, CTX_LEN, D_MODEL), 1.0),
        "c_ctx": nrm(ks[3], (D_MODEL,), 1.0),
        "w_mod": nrm(ks[4], (DEPTH, D_MODEL, N_MOD * D_MODEL), 0.5 * D_MODEL ** -0.5),
        "b_mod": nrm(ks[5], (DEPTH, N_MOD * D_MODEL), 0.01),
        "g_norm": 1.0 + nrm(ks[6], (DEPTH, 3, D_MODEL), 0.02),
        "w_ffn_gu": nrm(ks[7], (DEPTH, 2, D_MODEL, 2 * D_FF), D_MODEL ** -0.5),
        "w_ffn_down": nrm(ks[8], (DEPTH, 2, D_FF, D_MODEL), D_FF ** -0.5),
        "w_in": nrm(ks[9], (DEPTH, D_MODEL, IN_COLS), D_MODEL ** -0.5),
        "lambda_q1": nrm(ks[10], (DEPTH, HEAD_DIM), 0.1),
        "lambda_k1": nrm(ks[11], (DEPTH, HEAD_DIM), 0.1),
        "lambda_q2": nrm(ks[12], (DEPTH, HEAD_DIM), 0.1),
        "lambda_k2": nrm(ks[13], (DEPTH, HEAD_DIM), 0.1),
        "g_subln": 1.0 + nrm(ks[14], (DEPTH, V_DIM), 0.02),
        "w_pool": nrm(ks[15], (DEPTH, POOL_GROUPS, POOL_GROUP_DIM, POOL_GROUP_DIM), POOL_GROUP_DIM ** -0.5),
        "pool_scale": 1.0 + nrm(ks[16], (DEPTH, POOL_WIDTH), 0.02),
        "w_branch_attn": nrm(ks[17], (DEPTH, ATTN_WIDTH, D_MODEL), ATTN_WIDTH ** -0.5),
        "w_branch_pool": nrm(ks[18], (DEPTH, POOL_WIDTH, D_MODEL), POOL_WIDTH ** -0.5),
        "w_out": nrm(ks[19], (DEPTH, D_MODEL, D_MODEL), D_MODEL ** -0.5),
        "g_final": 1.0 + nrm(ks[20], (D_MODEL,), 0.02),
    }


def reference(x, c, ctx, c_ctx, w_mod, b_mod, g_norm, w_ffn_gu, w_ffn_down, w_in,
              lambda_q1, lambda_k1, lambda_q2, lambda_k2, g_subln, w_pool, pool_scale,
              w_branch_attn, w_branch_pool, w_out, g_final):
    b, l, _ = x.shape
    rows = l // GRID_W
    row_ids = jnp.repeat(jnp.arange(rows), GRID_W)
    col_ids = jnp.tile(jnp.arange(GRID_W), rows)
    n_blocks = l // Q_BLOCK

    lat, cx = x, ctx
    for layer in range(DEPTH):
        last = layer == DEPTH - 1
        mod_lat = (jax.nn.silu(c) @ w_mod[layer] + b_mod[layer])[:, None, :]
        mod_ctx = (jax.nn.silu(c_ctx) @ w_mod[layer] + b_mod[layer])[None, None, :]
        ml = jnp.split(mod_lat, N_MOD, axis=-1)
        mc = jnp.split(mod_ctx, N_MOD, axis=-1)

        lat = lat + 0.5 * ml[2] * swiglu(modulate(rms_norm(lat, g_norm[layer, 0]), ml[0], ml[1]),
                                         w_ffn_gu[layer, 0], w_ffn_down[layer, 0])
        cx = cx + 0.5 * mc[2] * swiglu(modulate(rms_norm(cx, g_norm[layer, 0]), mc[0], mc[1]),
                                       w_ffn_gu[layer, 0], w_ffn_down[layer, 0])

        h_lat = modulate(rms_norm(lat, g_norm[layer, 1]), ml[3], ml[4])
        h_ctx = modulate(rms_norm(cx, g_norm[layer, 1]), mc[3], mc[4])
        proj_lat = h_lat @ w_in[layer]
        q_l = split_qk(proj_lat[..., Q_OFF:K_OFF])
        k_l = split_qk(proj_lat[..., K_OFF:V_OFF])
        v_l = split_v(proj_lat[..., V_OFF:P_OFF])
        u_l = proj_lat[..., P_OFF:G_OFF]
        gates_l = proj_lat[..., G_OFF:]
        if last:
            proj_ctx = h_ctx @ w_in[layer][:, K_OFF:P_OFF]
            k_c = split_qk(proj_ctx[..., :QK_WIDTH])
            v_c = split_v(proj_ctx[..., QK_WIDTH:])
        else:
            proj_ctx = h_ctx @ w_in[layer]
            q_c = split_qk(proj_ctx[..., Q_OFF:K_OFF])
            k_c = split_qk(proj_ctx[..., K_OFF:V_OFF])
            v_c = split_v(proj_ctx[..., V_OFF:P_OFF])
            u_c = proj_ctx[..., P_OFF:G_OFF]
            gates_c = proj_ctx[..., G_OFF:]

        q_l = rope_2d(q_l, row_ids, col_ids)
        k_l = rope_2d(k_l, row_ids, col_ids)
        k_all = jnp.concatenate([k_c, k_l], axis=1)
        v_all = jnp.concatenate([v_c, v_l], axis=1)

        lam_init = 0.8 - 0.6 * math.exp(-0.3 * layer)
        lam = (jnp.exp(jnp.sum(lambda_q1[layer].astype(jnp.float32) * lambda_k1[layer].astype(jnp.float32)))
               - jnp.exp(jnp.sum(lambda_q2[layer].astype(jnp.float32) * lambda_k2[layer].astype(jnp.float32)))
               + lam_init)

        q_blocks = jnp.moveaxis(q_l.reshape(b, n_blocks, Q_BLOCK, N_HEADS, 2, HEAD_DIM), 1, 0)
        o_blocks = lax.map(lambda qb: diff_attn_block(qb, k_all, v_all, lam), q_blocks)
        o_lat = jnp.moveaxis(o_blocks, 0, 1).reshape(b, l, N_HEADS, V_DIM)
        attn_lat = diff_attn_post(o_lat, g_subln[layer], lam_init)
        pool_lat = pool_mix(u_l, w_pool[layer], pool_scale[layer])
        lat = lat + ml[5] * branch_merge(attn_lat, pool_lat, gates_l, w_branch_attn[layer],
                                         w_branch_pool[layer], w_out[layer])

        if not last:
            attn_ctx = diff_attn_post(diff_attn_block(q_c, k_c, v_c, lam), g_subln[layer], lam_init)
            pool_ctx = pool_mix(u_c, w_pool[layer], pool_scale[layer])
            cx = cx + mc[5] * branch_merge(attn_ctx, pool_ctx, gates_c, w_branch_attn[layer],
                                           w_branch_pool[layer], w_out[layer])
            cx = cx + 0.5 * mc[8] * swiglu(modulate(rms_norm(cx, g_norm[layer, 2]), mc[6], mc[7]),
                                           w_ffn_gu[layer, 1], w_ffn_down[layer, 1])

        lat = lat + 0.5 * ml[8] * swiglu(modulate(rms_norm(lat, g_norm[layer, 2]), ml[6], ml[7]),
                                         w_ffn_gu[layer, 1], w_ffn_down[layer, 1])

    return rms_norm(lat, g_final)
```

```python
import functools
import math

import jax
import jax.numpy as jnp
from jax import lax
from jax.experimental import pallas as pl
from jax.experimental.pallas import tpu as pltpu

F32 = jnp.float32
BF16 = jnp.bfloat16

D_MODEL = 1024
N_HEADS = 8
HEAD_DIM = 64
V_DIM = 2 * HEAD_DIM
QK_WIDTH = N_HEADS * 2 * HEAD_DIM
ATTN_WIDTH = N_HEADS * V_DIM
POOL_WINDOWS = (2, 4, 8, 16)
POOL_GROUP_DIM = 128
POOL_WIDTH = len(POOL_WINDOWS) * POOL_GROUP_DIM
POOL_HALO = 8
D_FF = 2816
GRID_W = 64
ROPE_BASE = 10000.0
ROPE_AXIS_DIM = HEAD_DIM // 2
N_MOD = 9
EPS = 1e-6
LAM_INIT = 0.8 - 0.6 * math.exp(-0.3 * 0)

Q_OFF = 0
K_OFF = Q_OFF + QK_WIDTH
V_OFF = K_OFF + QK_WIDTH
P_OFF = V_OFF + ATTN_WIDTH
G_OFF = P_OFF + POOL_WIDTH
IN_COLS = G_OFF + 2 * D_MODEL

LOG2E = 1.4426950408889634

VMEM_LIMIT_BYTES = 56 * 1024 * 1024
TOKEN_TILE = 512
FF_CHUNK = 256
Q_TILE = 256
MOD_ROWS = 24


def _sigmoid(x):
    return 1.0 / (1.0 + jnp.exp(-x))


def _rms_norm(x, g):
    return x * lax.rsqrt(jnp.mean(x * x, axis=-1, keepdims=True) + EPS) * g


def _mm(a, b):
    return jnp.dot(a, b, preferred_element_type=F32)


def _swiglu(h, wgu_ref, wd_ref):
    acc = None
    for j in range(D_FF // FF_CHUNK):
        lo = j * FF_CHUNK
        a = _mm(h, wgu_ref[:, lo:lo + FF_CHUNK])
        b = _mm(h, wgu_ref[:, D_FF + lo:D_FF + lo + FF_CHUNK])
        t = (a * _sigmoid(a) * b).astype(BF16)
        part = _mm(t, wd_ref[lo:lo + FF_CHUNK, :])
        acc = part if acc is None else acc + part
    return acc


def _mod_kernel(c_ref, w_ref, b_ref, o_ref):
    c = c_ref[...]
    s = (c * _sigmoid(c)).astype(BF16)
    o_ref[...] = _mm(s, w_ref[...].astype(BF16)) + b_ref[...]


def _modulation(cc, w_mod, b_mod):
    n_out = w_mod.shape[1]
    blk = D_MODEL
    return pl.pallas_call(
        _mod_kernel,
        grid=(n_out // blk,),
        in_specs=[
            pl.BlockSpec((MOD_ROWS, D_MODEL), lambda j: (0, 0)),
            pl.BlockSpec((D_MODEL, blk), lambda j: (0, j)),
            pl.BlockSpec((1, blk), lambda j: (0, j)),
        ],
        out_specs=pl.BlockSpec((MOD_ROWS, blk), lambda j: (0, j)),
        out_shape=jax.ShapeDtypeStruct((MOD_ROWS, n_out), F32),
        compiler_params=pltpu.CompilerParams(
            dimension_semantics=("arbitrary",), vmem_limit_bytes=VMEM_LIMIT_BYTES),
        name="mod",
    )(cc, w_mod, b_mod)


def _mod_spec(row_of_tile, chunk):
    return pl.BlockSpec((None, 1, D_MODEL), lambda i: (row_of_tile(i), 0, chunk))


def _resident(shape):
    nd = len(shape)
    return pl.BlockSpec(shape, lambda i: (0,) * nd, pipeline_mode=pl.Buffered(1))


def _row_spec(width):
    return pl.BlockSpec((TOKEN_TILE, width), lambda i: (i, 0))


def _vec_spec(idx):
    return pl.BlockSpec((None, 1, D_MODEL), lambda i: (idx, 0, 0))


def _ffn_kernel(x_ref, shift_ref, scale_ref, gate_ref, g_ref, wgu_ref, wd_ref, o_ref):
    x = x_ref[...]
    h = (_rms_norm(x, g_ref[...]) * (1.0 + scale_ref[...]) + shift_ref[...]).astype(BF16)
    o_ref[...] = x + (0.5 * gate_ref[...]) * _swiglu(h, wgu_ref, wd_ref)


def _ffn(x, mod3, row_of_tile, g3, g_idx, chunk0, wgu, wd):
    n_tok = x.shape[0]
    return pl.pallas_call(
        _ffn_kernel,
        grid=(n_tok // TOKEN_TILE,),
        in_specs=[
            _row_spec(D_MODEL),
            _mod_spec(row_of_tile, chunk0),
            _mod_spec(row_of_tile, chunk0 + 1),
            _mod_spec(row_of_tile, chunk0 + 2),
            _vec_spec(g_idx),
            _resident(wgu.shape),
            _resident(wd.shape),
        ],
        out_specs=_row_spec(D_MODEL),
        out_shape=jax.ShapeDtypeStruct((n_tok, D_MODEL), F32),
        compiler_params=pltpu.CompilerParams(
            dimension_semantics=("parallel",), vmem_limit_bytes=VMEM_LIMIT_BYTES),
        name="ffn",
    )(x, mod3, mod3, mod3, g3, wgu, wd)


def _rope(z, cos, sin_lo, sin_hi):
    return z * cos + pltpu.roll(z, V_DIM - 16, 1) * sin_lo + pltpu.roll(z, 16, 1) * sin_hi


def _mix_in_lat_kernel(x_ref, shift_ref, scale_ref, g_ref, w_ref, cos_ref, slo_ref, shi_ref,
                       q_ref, k_ref, v_ref, u_ref, sg_ref):
    h = (_rms_norm(x_ref[...], g_ref[...]) * (1.0 + scale_ref[...]) + shift_ref[...]).astype(BF16)
    cos, slo, shi = cos_ref[...], slo_ref[...], shi_ref[...]
    q = _mm(h, w_ref[:, Q_OFF:K_OFF])
    k = _mm(h, w_ref[:, K_OFF:V_OFF])
    q_scale = HEAD_DIM ** -0.5 * LOG2E
    for hd in range(N_HEADS):
        sl = slice(hd * V_DIM, (hd + 1) * V_DIM)
        q_ref[:, sl] = (_rope(q[:, sl], cos, slo, shi) * q_scale).astype(BF16)
        k_ref[:, sl] = _rope(k[:, sl], cos, slo, shi).astype(BF16)
    v_ref[...] = _mm(h, w_ref[:, V_OFF:P_OFF]).astype(BF16)
    u_ref[...] = _mm(h, w_ref[:, P_OFF:G_OFF])
    sg_ref[...] = _sigmoid(_mm(h, w_ref[:, G_OFF:IN_COLS])).astype(BF16)


def _mix_in_lat(x, mod3, row_of_tile, g3, w_in, tables, seq_len):
    n_tok = x.shape[0]
    tiles_per_seq = seq_len // TOKEN_TILE
    tab_spec = pl.BlockSpec((TOKEN_TILE, V_DIM), lambda i: (i % tiles_per_seq, 0))
    bf = lambda w: jax.ShapeDtypeStruct((n_tok, w), BF16)
    return pl.pallas_call(
        _mix_in_lat_kernel,
        grid=(n_tok // TOKEN_TILE,),
        in_specs=[
            _row_spec(D_MODEL),
            _mod_spec(row_of_tile, 3),
            _mod_spec(row_of_tile, 4),
            _vec_spec(1),
            _resident(w_in.shape),
            tab_spec, tab_spec, tab_spec,
        ],
        out_specs=[_row_spec(QK_WIDTH), _row_spec(QK_WIDTH), _row_spec(ATTN_WIDTH),
                   _row_spec(POOL_WIDTH), _row_spec(2 * D_MODEL)],
        out_shape=[bf(QK_WIDTH), bf(QK_WIDTH), bf(ATTN_WIDTH),
                   jax.ShapeDtypeStruct((n_tok, POOL_WIDTH), F32), bf(2 * D_MODEL)],
        compiler_params=pltpu.CompilerParams(
            dimension_semantics=("parallel",), vmem_limit_bytes=VMEM_LIMIT_BYTES),
        name="mix_in_lat",
    )(x, mod3, mod3, g3, w_in, *tables)


def _mix_in_ctx_kernel(x_ref, shift_ref, scale_ref, g_ref, w_ref, k_ref, v_ref):
    h = (_rms_norm(x_ref[...], g_ref[...]) * (1.0 + scale_ref[...]) + shift_ref[...]).astype(BF16)
    k_ref[...] = _mm(h, w_ref[:, :QK_WIDTH]).astype(BF16)
    v_ref[...] = _mm(h, w_ref[:, QK_WIDTH:]).astype(BF16)


def _mix_in_ctx(x, mod3, row_of_tile, g3, w_kv):
    n_tok = x.shape[0]
    return pl.pallas_call(
        _mix_in_ctx_kernel,
        grid=(n_tok // TOKEN_TILE,),
        in_specs=[
            _row_spec(D_MODEL),
            _mod_spec(row_of_tile, 3),
            _mod_spec(row_of_tile, 4),
            _vec_spec(1),
            _resident(w_kv.shape),
        ],
        out_specs=[_row_spec(QK_WIDTH), _row_spec(ATTN_WIDTH)],
        out_shape=[jax.ShapeDtypeStruct((n_tok, QK_WIDTH), BF16),
                   jax.ShapeDtypeStruct((n_tok, ATTN_WIDTH), BF16)],
        compiler_params=pltpu.CompilerParams(
            dimension_semantics=("parallel",), vmem_limit_bytes=VMEM_LIMIT_BYTES),
        name="mix_in_ctx",
    )(x, mod3, mod3, g3, w_kv)


def _attn_kernel(lq1_ref, lk1_ref, lq2_ref, lk2_ref, gs_ref, q_ref, kc_ref, kl_ref, vc_ref, vl_ref,
                 o_ref):
    lam = (jnp.exp(jnp.sum(lq1_ref[...] * lk1_ref[...], axis=-1, keepdims=True))
           - jnp.exp(jnp.sum(lq2_ref[...] * lk2_ref[...], axis=-1, keepdims=True))
           + LAM_INIT)
    gs = gs_ref[...] * (1.0 - LAM_INIT)
    kc, kl, vc, vl = kc_ref[...], kl_ref[...], vc_ref[...], vl_ref[...]
    first_comp = lax.broadcasted_iota(jnp.int32, (Q_TILE, V_DIM), 1) < HEAD_DIM
    nt = (((1,), (1,)), ((), ()))

    def block(t, carry):
        rows = pl.ds(pl.multiple_of(t * Q_TILE, Q_TILE), Q_TILE)
        q = q_ref[rows, :]
        zero = jnp.zeros_like(q)
        qq = jnp.concatenate([jnp.where(first_comp, q, zero), jnp.where(first_comp, zero, q)], axis=0)
        sc = lax.dot_general(qq, kc, nt, preferred_element_type=F32)
        sl = lax.dot_general(qq, kl, nt, preferred_element_type=F32)
        m = jnp.maximum(jnp.max(sc, axis=-1, keepdims=True), jnp.max(sl, axis=-1, keepdims=True))
        pc = jnp.exp2(sc - m)
        pl_ = jnp.exp2(sl - m)
        inv = 1.0 / (jnp.sum(pc, axis=-1, keepdims=True) + jnp.sum(pl_, axis=-1, keepdims=True))
        a1 = inv[:Q_TILE]
        a2 = inv[Q_TILE:] * lam
        dc = (pc[:Q_TILE] * a1 - pc[Q_TILE:] * a2).astype(BF16)
        dl = (pl_[:Q_TILE] * a1 - pl_[Q_TILE:] * a2).astype(BF16)
        o = _mm(dc, vc) + _mm(dl, vl)
        o_ref[rows, :] = _rms_norm(o, gs).astype(BF16)
        return carry

    lax.fori_loop(0, q_ref.shape[0] // Q_TILE, block, 0)


def _attention(lams, gs, q, kc, kl, vc, vl):
    bsz, seq_len, _ = q.shape
    ctx_len = kc.shape[1]
    lam_spec = pl.BlockSpec((1, HEAD_DIM), lambda b, h: (0, 0))
    lat_spec = pl.BlockSpec((None, seq_len, V_DIM), lambda b, h: (b, 0, h))
    ctx_spec = pl.BlockSpec((None, ctx_len, V_DIM), lambda b, h: (b, 0, h))
    return pl.pallas_call(
        _attn_kernel,
        grid=(bsz, N_HEADS),
        in_specs=[lam_spec, lam_spec, lam_spec, lam_spec,
                  pl.BlockSpec((1, V_DIM), lambda b, h: (0, 0)),
                  lat_spec, ctx_spec, lat_spec, ctx_spec, lat_spec],
        out_specs=lat_spec,
        out_shape=jax.ShapeDtypeStruct((bsz, seq_len, ATTN_WIDTH), BF16),
        compiler_params=pltpu.CompilerParams(
            dimension_semantics=("parallel", "parallel"), vmem_limit_bytes=VMEM_LIMIT_BYTES),
        name="attn",
    )(*lams, gs, q, kc, kl, vc, vl)


def _pool_branch(u, u_prev, u_next, wpool_ref, pscale_ref, tiles_per_seq, seq_len):
    i = pl.program_id(0)
    pos0 = (i % tiles_per_seq) * TOKEN_TILE
    u_prev = jnp.where(pos0 > 0, u_prev, 0.0)
    u_next = jnp.where(pos0 + TOKEN_TILE < seq_len, u_next, 0.0)
    ext = jnp.concatenate([u_prev, u, u_next], axis=0)
    n_ext = ext.shape[0]
    pos = pos0 + lax.broadcasted_iota(jnp.int32, (TOKEN_TILE, POOL_GROUP_DIM), 0)
    outs = []
    for g, w in enumerate(POOL_WINDOWS):
        cols = slice(g * POOL_GROUP_DIM, (g + 1) * POOL_GROUP_DIM)
        win = ext[:, cols]
        win = win + pltpu.roll(win, 1, 0)
        span = 1
        while 2 * span < w:
            win = pltpu.roll(win, span, 0) + pltpu.roll(win, n_ext - span, 0)
            span *= 2
        win = win[POOL_HALO:POOL_HALO + TOKEN_TILE]
        cnt = (jnp.minimum(pos + w // 2, seq_len) - jnp.maximum(pos - w // 2, 0)).astype(F32)
        pooled = (win / cnt - u[:, cols]).astype(BF16)
        outs.append(_mm(pooled, wpool_ref[g]))
    return jnp.concatenate(outs, axis=-1) * pscale_ref[...]


def _merge_kernel(x_ref, attn_ref, u_ref, up_ref, un_ref, sg_ref,
                  gate_mix_ref, shift_ref, scale_ref, gate_ffn_ref, g_ref, gfin_ref,
                  wpool_ref, pscale_ref, wba_ref, wbp_ref, wout_ref, wgu_ref, wd_ref,
                  o_ref, *, tiles_per_seq, seq_len):
    pool = _pool_branch(u_ref[...], up_ref[...], un_ref[...], wpool_ref, pscale_ref,
                        tiles_per_seq, seq_len)
    y = (sg_ref[:, :D_MODEL].astype(F32) * _mm(attn_ref[...], wba_ref[...])
         + sg_ref[:, D_MODEL:].astype(F32) * _mm(pool.astype(BF16), wbp_ref[...]))
    x = x_ref[...] + gate_mix_ref[...] * _mm(y.astype(BF16), wout_ref[...])
    h = (_rms_norm(x, g_ref[...]) * (1.0 + scale_ref[...]) + shift_ref[...]).astype(BF16)
    x = x + (0.5 * gate_ffn_ref[...]) * _swiglu(h, wgu_ref, wd_ref)
    o_ref[...] = _rms_norm(x, gfin_ref[...])


def _merge(x, attn, u, sg, mod3, row_of_tile, g3, g_final, w_pool, pool_scale, w_ba, w_bp, w_out,
           wgu, wd, seq_len):
    n_tok = x.shape[0]
    tiles_per_seq = seq_len // TOKEN_TILE
    halo_blocks = TOKEN_TILE // POOL_HALO
    n_halo = n_tok // POOL_HALO
    prev_spec = pl.BlockSpec((POOL_HALO, POOL_WIDTH),
                             lambda i: (jnp.maximum(i * halo_blocks - 1, 0), 0))
    next_spec = pl.BlockSpec((POOL_HALO, POOL_WIDTH),
                             lambda i: (jnp.minimum((i + 1) * halo_blocks, n_halo - 1), 0))
    kern = functools.partial(_merge_kernel, tiles_per_seq=tiles_per_seq, seq_len=seq_len)
    return pl.pallas_call(
        kern,
        grid=(n_tok // TOKEN_TILE,),
        in_specs=[
            _row_spec(D_MODEL), _row_spec(ATTN_WIDTH), _row_spec(POOL_WIDTH), prev_spec, next_spec,
            _row_spec(2 * D_MODEL),
            _mod_spec(row_of_tile, 5), _mod_spec(row_of_tile, 6), _mod_spec(row_of_tile, 7),
            _mod_spec(row_of_tile, 8), _vec_spec(2),
            pl.BlockSpec((1, D_MODEL), lambda i: (0, 0)),
            _resident(w_pool.shape), pl.BlockSpec((1, POOL_WIDTH), lambda i: (0, 0)),
            _resident(w_ba.shape), _resident(w_bp.shape), _resident(w_out.shape),
            _resident(wgu.shape), _resident(wd.shape),
        ],
        out_specs=_row_spec(D_MODEL),
        out_shape=jax.ShapeDtypeStruct((n_tok, D_MODEL), F32),
        compiler_params=pltpu.CompilerParams(
            dimension_semantics=("parallel",), vmem_limit_bytes=VMEM_LIMIT_BYTES),
        name="merge",
    )(x, attn, u, u, u, sg, mod3, mod3, mod3, mod3, g3, g_final,
      w_pool, pool_scale, w_ba, w_bp, w_out, wgu, wd)


def _rope_tables(seq_len):
    lane = jnp.arange(V_DIM)
    within = lane % HEAD_DIM
    on_row_axis = within < ROPE_AXIS_DIM
    half = ROPE_AXIS_DIM // 2
    low_half = (within % ROPE_AXIS_DIM) < half
    freqs = ROPE_BASE ** (-jnp.arange(half, dtype=F32) / half)
    t = jnp.arange(seq_len)
    pos = jnp.where(on_row_axis[None, :], (t // GRID_W)[:, None], (t % GRID_W)[:, None]).astype(F32)
    ang = pos * freqs[within % half][None, :]
    cos, sin = jnp.cos(ang), jnp.sin(ang)
    zero = jnp.zeros_like(sin)
    return cos, jnp.where(low_half[None, :], -sin, zero), jnp.where(low_half[None, :], zero, sin)


def kernel(x, c, ctx, c_ctx, w_mod, b_mod, g_norm, w_ffn_gu, w_ffn_down, w_in, lambda_q1, lambda_k1,
           lambda_q2, lambda_k2, g_subln, w_pool, pool_scale, w_branch_attn, w_branch_pool, w_out,
           g_final):
    bsz, seq_len, _ = x.shape
    ctx_len = ctx.shape[1]
    assert w_mod.shape[0] == 1, "single-layer block"
    assert seq_len % TOKEN_TILE == 0 and (bsz * ctx_len) % TOKEN_TILE == 0 and bsz + 1 <= MOD_ROWS

    cc = jnp.concatenate([c, c_ctx[None, :], jnp.zeros((MOD_ROWS - bsz - 1, D_MODEL), F32)], axis=0)
    mod3 = _modulation(cc, w_mod[0], b_mod).reshape(MOD_ROWS, 1, N_MOD * D_MODEL)
    g3 = g_norm[0].reshape(3, 1, D_MODEL)

    wgu = w_ffn_gu[0].astype(BF16)
    wd = w_ffn_down[0].astype(BF16)
    w_in_b = w_in[0].astype(BF16)

    tiles_per_seq = seq_len // TOKEN_TILE
    lat_row = lambda i: i // tiles_per_seq
    ctx_row = lambda i: bsz

    lat = _ffn(x.reshape(bsz * seq_len, D_MODEL), mod3, lat_row, g3, 0, 0, wgu[0], wd[0])
    cx = _ffn(ctx.reshape(bsz * ctx_len, D_MODEL), mod3, ctx_row, g3, 0, 0, wgu[0], wd[0])

    q, k_l, v_l, u, sg = _mix_in_lat(lat, mod3, lat_row, g3, w_in_b, _rope_tables(seq_len), seq_len)
    k_c, v_c = _mix_in_ctx(cx, mod3, ctx_row, g3, w_in_b[:, K_OFF:P_OFF])

    lams = [v.reshape(1, HEAD_DIM) for v in (lambda_q1, lambda_k1, lambda_q2, lambda_k2)]
    attn = _attention(
        lams, g_subln.reshape(1, V_DIM),
        q.reshape(bsz, seq_len, QK_WIDTH),
        k_c.reshape(bsz, ctx_len, QK_WIDTH), k_l.reshape(bsz, seq_len, QK_WIDTH),
        v_c.reshape(bsz, ctx_len, ATTN_WIDTH), v_l.reshape(bsz, seq_len, ATTN_WIDTH))

    out = _merge(lat, attn.reshape(bsz * seq_len, ATTN_WIDTH), u, sg, mod3, lat_row, g3,
                 g_final.reshape(1, D_MODEL), w_pool[0].astype(BF16), pool_scale.reshape(1, POOL_WIDTH),
                 w_branch_attn[0].astype(BF16), w_branch_pool[0].astype(BF16), w_out[0].astype(BF16),
                 wgu[1], wd[1], seq_len)
    return out.reshape(bsz, seq_len, D_MODEL)
```

```python
import functools
import math

import jax
import jax.numpy as jnp
from jax import lax
from jax.experimental import pallas as pl
from jax.experimental.pallas import tpu as pltpu

F32 = jnp.float32
BF16 = jnp.bfloat16

D_MODEL = 1024
N_HEADS = 8
HEAD_DIM = 64
V_DIM = 2 * HEAD_DIM
QK_WIDTH = N_HEADS * 2 * HEAD_DIM
ATTN_WIDTH = N_HEADS * V_DIM
POOL_WINDOWS = (2, 4, 8, 16)
POOL_GROUP_DIM = 128
POOL_WIDTH = len(POOL_WINDOWS) * POOL_GROUP_DIM
POOL_HALO = 8
D_FF = 2816
GRID_W = 64
ROPE_BASE = 10000.0
ROPE_AXIS_DIM = HEAD_DIM // 2
N_MOD = 9
EPS = 1e-6
LAM_INIT = 0.8 - 0.6 * math.exp(-0.3 * 0)

Q_OFF = 0
K_OFF = Q_OFF + QK_WIDTH
V_OFF = K_OFF + QK_WIDTH
P_OFF = V_OFF + ATTN_WIDTH
G_OFF = P_OFF + POOL_WIDTH
IN_COLS = G_OFF + 2 * D_MODEL

LOG2E = 1.4426950408889634

VMEM_LIMIT_BYTES = 56 * 1024 * 1024
TOKEN_TILE = 512
FF_CHUNK = 256
Q_TILE = 256
MOD_ROWS = 24


def _sigmoid(x):
    return 1.0 / (1.0 + jnp.exp(-x))


def _rms_norm(x, g):
    return x * lax.rsqrt(jnp.mean(x * x, axis=-1, keepdims=True) + EPS) * g


def _mm(a, b):
    return jnp.dot(a, b, preferred_element_type=F32)


def _swiglu(h, wgu_ref, wd_ref):
    acc = None
    for j in range(D_FF // FF_CHUNK):
        lo = j * FF_CHUNK
        a = _mm(h, wgu_ref[:, lo:lo + FF_CHUNK])
        b = _mm(h, wgu_ref[:, D_FF + lo:D_FF + lo + FF_CHUNK])
        t = (a * _sigmoid(a) * b).astype(BF16)
        part = _mm(t, wd_ref[lo:lo + FF_CHUNK, :])
        acc = part if acc is None else acc + part
    return acc


def _mod_kernel(c_ref, w_ref, b_ref, o_ref):
    c = c_ref[...]
    s = (c * _sigmoid(c)).astype(BF16)
    o_ref[...] = _mm(s, w_ref[...].astype(BF16)) + b_ref[...]


def _modulation(cc, w_mod, b_mod):
    n_out = w_mod.shape[1]
    blk = D_MODEL
    return pl.pallas_call(
        _mod_kernel,
        grid=(n_out // blk,),
        in_specs=[
            pl.BlockSpec((MOD_ROWS, D_MODEL), lambda j: (0, 0)),
            pl.BlockSpec((D_MODEL, blk), lambda j: (0, j)),
            pl.BlockSpec((1, blk), lambda j: (0, j)),
        ],
        out_specs=pl.BlockSpec((MOD_ROWS, blk), lambda j: (0, j)),
        out_shape=jax.ShapeDtypeStruct((MOD_ROWS, n_out), F32),
        compiler_params=pltpu.CompilerParams(
            dimension_semantics=("arbitrary",), vmem_limit_bytes=VMEM_LIMIT_BYTES),
        name="mod",
    )(cc, w_mod, b_mod)


def _mod_spec(row_of_tile, chunk):
    return pl.BlockSpec((None, 1, D_MODEL), lambda i: (row_of_tile(i), 0, chunk))


def _resident(shape):
    nd = len(shape)
    return pl.BlockSpec(shape, lambda i: (0,) * nd, pipeline_mode=pl.Buffered(1))


def _row_spec(width):
    return pl.BlockSpec((TOKEN_TILE, width), lambda i: (i, 0))


def _vec_spec(idx):
    return pl.BlockSpec((None, 1, D_MODEL), lambda i: (idx, 0, 0))


def _ffn_kernel(x_ref, shift_ref, scale_ref, gate_ref, g_ref, wgu_ref, wd_ref, o_ref):
    x = x_ref[...]
    h = (_rms_norm(x, g_ref[...]) * (1.0 + scale_ref[...]) + shift_ref[...]).astype(BF16)
    o_ref[...] = x + (0.5 * gate_ref[...]) * _swiglu(h, wgu_ref, wd_ref)


def _ffn(x, mod3, row_of_tile, g3, g_idx, chunk0, wgu, wd):
    n_tok = x.shape[0]
    return pl.pallas_call(
        _ffn_kernel,
        grid=(n_tok // TOKEN_TILE,),
        in_specs=[
            _row_spec(D_MODEL),
            _mod_spec(row_of_tile, chunk0),
            _mod_spec(row_of_tile, chunk0 + 1),
            _mod_spec(row_of_tile, chunk0 + 2),
            _vec_spec(g_idx),
            _resident(wgu.shape),
            _resident(wd.shape),
        ],
        out_specs=_row_spec(D_MODEL),
        out_shape=jax.ShapeDtypeStruct((n_tok, D_MODEL), F32),
        compiler_params=pltpu.CompilerParams(
            dimension_semantics=("parallel",), vmem_limit_bytes=VMEM_LIMIT_BYTES),
        name="ffn",
    )(x, mod3, mod3, mod3, g3, wgu, wd)


def _rope(z, cos, sin_lo, sin_hi):
    return z * cos + pltpu.roll(z, V_DIM - 16, 1) * sin_lo + pltpu.roll(z, 16, 1) * sin_hi


def _mix_in_lat_kernel(x_ref, shift_ref, scale_ref, g_ref, w_ref, cos_ref, slo_ref, shi_ref,
                       q_ref, k_ref, v_ref, u_ref, sg_ref):
    h = (_rms_norm(x_ref[...], g_ref[...]) * (1.0 + scale_ref[...]) + shift_ref[...]).astype(BF16)
    cos, slo, shi = cos_ref[...], slo_ref[...], shi_ref[...]
    q = _mm(h, w_ref[:, Q_OFF:K_OFF])
    k = _mm(h, w_ref[:, K_OFF:V_OFF])
    q_scale = HEAD_DIM ** -0.5 * LOG2E
    for hd in range(N_HEADS):
        sl = slice(hd * V_DIM, (hd + 1) * V_DIM)
        q_ref[:, sl] = (_rope(q[:, sl], cos, slo, shi) * q_scale).astype(BF16)
        k_ref[:, sl] = _rope(k[:, sl], cos, slo, shi).astype(BF16)
    v_ref[...] = _mm(h, w_ref[:, V_OFF:P_OFF]).astype(BF16)
    u_ref[...] = _mm(h, w_ref[:, P_OFF:G_OFF])
    sg_ref[...] = _sigmoid(_mm(h, w_ref[:, G_OFF:IN_COLS])).astype(BF16)


def _mix_in_lat(x, mod3, row_of_tile, g3, w_in, tables, seq_len):
    n_tok = x.shape[0]
    tiles_per_seq = seq_len // TOKEN_TILE
    tab_spec = pl.BlockSpec((TOKEN_TILE, V_DIM), lambda i: (i % tiles_per_seq, 0))
    bf = lambda w: jax.ShapeDtypeStruct((n_tok, w), BF16)
    return pl.pallas_call(
        _mix_in_lat_kernel,
        grid=(n_tok // TOKEN_TILE,),
        in_specs=[
            _row_spec(D_MODEL),
            _mod_spec(row_of_tile, 3),
            _mod_spec(row_of_tile, 4),
            _vec_spec(1),
            _resident(w_in.shape),
            tab_spec, tab_spec, tab_spec,
        ],
        out_specs=[_row_spec(QK_WIDTH), _row_spec(QK_WIDTH), _row_spec(ATTN_WIDTH),
                   _row_spec(POOL_WIDTH), _row_spec(2 * D_MODEL)],
        out_shape=[bf(QK_WIDTH), bf(QK_WIDTH), bf(ATTN_WIDTH),
                   jax.ShapeDtypeStruct((n_tok, POOL_WIDTH), F32), bf(2 * D_MODEL)],
        compiler_params=pltpu.CompilerParams(
            dimension_semantics=("parallel",), vmem_limit_bytes=VMEM_LIMIT_BYTES),
        name="mix_in_lat",
    )(x, mod3, mod3, g3, w_in, *tables)


def _mix_in_ctx_kernel(x_ref, shift_ref, scale_ref, g_ref, wk_ref, wv_ref, k_ref, v_ref):
    h = (_rms_norm(x_ref[...], g_ref[...]) * (1.0 + scale_ref[...]) + shift_ref[...]).astype(BF16)
    k_ref[...] = _mm(h, wk_ref[...]).astype(BF16)
    v_ref[...] = _mm(h, wv_ref[...]).astype(BF16)


def _mix_in_ctx(x, mod3, row_of_tile, g3, w_in):
    n_tok = x.shape[0]
    assert K_OFF % QK_WIDTH == 0 and V_OFF % ATTN_WIDTH == 0
    col_block = lambda off, width: pl.BlockSpec(
        (D_MODEL, width), lambda i: (0, off // width), pipeline_mode=pl.Buffered(1))
    return pl.pallas_call(
        _mix_in_ctx_kernel,
        grid=(n_tok // TOKEN_TILE,),
        in_specs=[
            _row_spec(D_MODEL),
            _mod_spec(row_of_tile, 3),
            _mod_spec(row_of_tile, 4),
            _vec_spec(1),
            col_block(K_OFF, QK_WIDTH),
            col_block(V_OFF, ATTN_WIDTH),
        ],
        out_specs=[_row_spec(QK_WIDTH), _row_spec(ATTN_WIDTH)],
        out_shape=[jax.ShapeDtypeStruct((n_tok, QK_WIDTH), BF16),
                   jax.ShapeDtypeStruct((n_tok, ATTN_WIDTH), BF16)],
        compiler_params=pltpu.CompilerParams(
            dimension_semantics=("parallel",), vmem_limit_bytes=VMEM_LIMIT_BYTES),
        name="mix_in_ctx",
    )(x, mod3, mod3, g3, w_in, w_in)


def _attn_kernel(lq1_ref, lk1_ref, lq2_ref, lk2_ref, gs_ref, q_ref, kc_ref, kl_ref, vc_ref, vl_ref,
                 o_ref, k_scr, v_scr, sa_ref, sb_ref, ma_ref, mb_ref, oa_ref, ob_ref):
    lam = (jnp.exp(jnp.sum(lq1_ref[...] * lk1_ref[...], axis=-1, keepdims=True))
           - jnp.exp(jnp.sum(lq2_ref[...] * lk2_ref[...], axis=-1, keepdims=True))
           + LAM_INIT)
    gs = gs_ref[...] * (1.0 - LAM_INIT)
    ctx_len = kc_ref.shape[0]
    n_keys = k_scr.shape[0]
    k_scr[:ctx_len, :] = kc_ref[...]
    k_scr[ctx_len:, :] = kl_ref[...]
    v_scr[:ctx_len, :V_DIM] = vc_ref[...]
    v_scr[ctx_len:, :V_DIM] = vl_ref[...]
    v_scr[:, V_DIM:] = jnp.ones((n_keys, V_DIM), BF16)
    first_comp = lax.broadcasted_iota(jnp.int32, (Q_TILE, V_DIM), 1) < HEAD_DIM
    nt = (((1,), (1,)), ((), ()))

    def rows_of(t):
        if isinstance(t, int):
            return pl.ds(t * Q_TILE, Q_TILE)
        return pl.ds(pl.multiple_of(t * Q_TILE, Q_TILE), Q_TILE)

    def scores(t, s_ref, m_ref):
        q = q_ref[rows_of(t), :]
        zero = jnp.zeros_like(q)
        qq = jnp.concatenate([jnp.where(first_comp, q, zero), jnp.where(first_comp, zero, q)], axis=0)
        s = lax.dot_general(qq, k_scr[...], nt, preferred_element_type=F32)
        s_ref[...] = s
        m_ref[...] = jnp.max(s, axis=-1, keepdims=True)

    def values(s_ref, m_ref, ov_ref):
        for comp in range(2):
            r = slice(comp * Q_TILE, (comp + 1) * Q_TILE)
            p = jnp.exp2(s_ref[r, :] - m_ref[r, :]).astype(BF16)
            ov_ref[r, :] = _mm(p, v_scr[...])

    def finish(t, ov_ref):
        on = ov_ref[:, :V_DIM] / ov_ref[:, V_DIM:]
        o = on[:Q_TILE] - lam * on[Q_TILE:]
        o_ref[rows_of(t), :] = _rms_norm(o, gs).astype(BF16)

    n_blocks = q_ref.shape[0] // Q_TILE
    assert n_blocks % 2 == 0 and n_blocks >= 4
    scores(0, sa_ref, ma_ref)
    scores(1, sb_ref, mb_ref)
    values(sa_ref, ma_ref, oa_ref)

    def pair(j, carry):
        scores(2 * j + 2, sa_ref, ma_ref)
        values(sb_ref, mb_ref, ob_ref)
        finish(2 * j, oa_ref)
        scores(2 * j + 3, sb_ref, mb_ref)
        values(sa_ref, ma_ref, oa_ref)
        finish(2 * j + 1, ob_ref)
        return carry

    lax.fori_loop(0, n_blocks // 2 - 1, pair, 0)
    values(sb_ref, mb_ref, ob_ref)
    finish(n_blocks - 2, oa_ref)
    finish(n_blocks - 1, ob_ref)


def _attention(lams, gs, q, kc, kl, vc, vl):
    bsz, seq_len, _ = q.shape
    ctx_len = kc.shape[1]
    n_keys = ctx_len + seq_len
    lam_spec =pl.BlockSpec((1, HEAD_DIM), lambda b, h: (0, 0))
    lat_spec = pl.BlockSpec((None, seq_len, V_DIM), lambda b, h: (b, 0, h))
    ctx_spec = pl.BlockSpec((None, ctx_len, V_DIM), lambda b, h: (b, 0, h))
    return pl.pallas_call(
        _attn_kernel,
        grid=(bsz, N_HEADS),
        in_specs=[lam_spec, lam_spec, lam_spec, lam_spec,
                  pl.BlockSpec((1, V_DIM), lambda b, h: (0, 0)),
                  lat_spec, ctx_spec, lat_spec, ctx_spec, lat_spec],
        out_specs=lat_spec,
        out_shape=jax.ShapeDtypeStruct((bsz, seq_len, ATTN_WIDTH), BF16),
        scratch_shapes=[pltpu.VMEM((n_keys, V_DIM), BF16),
                        pltpu.VMEM((n_keys, 2 * V_DIM), BF16),
                        pltpu.VMEM((2 * Q_TILE, n_keys), F32), pltpu.VMEM((2 * Q_TILE, n_keys), F32),
                        pltpu.VMEM((2 * Q_TILE, 1), F32), pltpu.VMEM((2 * Q_TILE, 1), F32),
                        pltpu.VMEM((2 * Q_TILE, 2 * V_DIM), F32), pltpu.VMEM((2 * Q_TILE, 2 * V_DIM), F32)],
        compiler_params=pltpu.CompilerParams(
            dimension_semantics=("parallel", "parallel"), vmem_limit_bytes=VMEM_LIMIT_BYTES),
        name="attn",
    )(*lams, gs, q, kc, kl, vc, vl)


def _pool_branch(u, u_prev, u_next, wpool_ref, pscale_ref, tiles_per_seq, seq_len):
    i = pl.program_id(0)
    pos0 = (i % tiles_per_seq) * TOKEN_TILE
    u_prev = jnp.where(pos0 > 0, u_prev, 0.0)
    u_next = jnp.where(pos0 + TOKEN_TILE < seq_len, u_next, 0.0)
    ext = jnp.concatenate([u_prev, u, u_next], axis=0)
    n_ext = ext.shape[0]
    pos = pos0 + lax.broadcasted_iota(jnp.int32, (TOKEN_TILE, POOL_GROUP_DIM), 0)
    outs = []
    for g, w in enumerate(POOL_WINDOWS):
        cols = slice(g * POOL_GROUP_DIM, (g + 1) * POOL_GROUP_DIM)
        win = ext[:, cols]
        win = win + pltpu.roll(win, 1, 0)
        span = 1
        while 2 * span < w:
            win = pltpu.roll(win, span, 0) + pltpu.roll(win, n_ext - span, 0)
            span *= 2
        win = win[POOL_HALO:POOL_HALO + TOKEN_TILE]
        cnt = (jnp.minimum(pos + w // 2, seq_len) - jnp.maximum(pos - w // 2, 0)).astype(F32)
        pooled = (win / cnt - u[:, cols]).astype(BF16)
        outs.append(_mm(pooled, wpool_ref[g]))
    return jnp.concatenate(outs, axis=-1) * pscale_ref[...]


def _merge_kernel(x_ref, attn_ref, u_ref, up_ref, un_ref, sg_ref,
                  gate_mix_ref, shift_ref, scale_ref, gate_ffn_ref, g_ref, gfin_ref,
                  wpool_ref, pscale_ref, wba_ref, wbp_ref, wout_ref, wgu_ref, wd_ref,
                  o_ref, *, tiles_per_seq, seq_len):
    pool = _pool_branch(u_ref[...], up_ref[...], un_ref[...], wpool_ref, pscale_ref,
                        tiles_per_seq, seq_len)
    y = (sg_ref[:, :D_MODEL].astype(F32) * _mm(attn_ref[...], wba_ref[...])
         + sg_ref[:, D_MODEL:].astype(F32) * _mm(pool.astype(BF16), wbp_ref[...]))
    x = x_ref[...] + gate_mix_ref[...] * _mm(y.astype(BF16), wout_ref[...])
    h = (_rms_norm(x, g_ref[...]) * (1.0 + scale_ref[...]) + shift_ref[...]).astype(BF16)
    x = x + (0.5 * gate_ffn_ref[...]) * _swiglu(h, wgu_ref, wd_ref)
    o_ref[...] = _rms_norm(x, gfin_ref[...])


def _merge(x, attn, u, sg, mod3, row_of_tile, g3, g_final, w_pool, pool_scale, w_ba, w_bp, w_out,
           wgu, wd, seq_len):
    n_tok = x.shape[0]
    tiles_per_seq = seq_len // TOKEN_TILE
    halo_blocks = TOKEN_TILE // POOL_HALO
    n_halo = n_tok // POOL_HALO
    prev_spec = pl.BlockSpec((POOL_HALO, POOL_WIDTH),
                             lambda i: (jnp.maximum(i * halo_blocks - 1, 0), 0))
    next_spec = pl.BlockSpec((POOL_HALO, POOL_WIDTH),
                             lambda i: (jnp.minimum((i + 1) * halo_blocks, n_halo - 1), 0))
    kern = functools.partial(_merge_kernel, tiles_per_seq=tiles_per_seq, seq_len=seq_len)
    return pl.pallas_call(
        kern,
        grid=(n_tok // TOKEN_TILE,),
        in_specs=[
            _row_spec(D_MODEL), _row_spec(ATTN_WIDTH), _row_spec(POOL_WIDTH), prev_spec, next_spec,
            _row_spec(2 * D_MODEL),
            _mod_spec(row_of_tile, 5), _mod_spec(row_of_tile, 6), _mod_spec(row_of_tile, 7),
            _mod_spec(row_of_tile, 8), _vec_spec(2),
            pl.BlockSpec((1, D_MODEL), lambda i: (0, 0)),
            _resident(w_pool.shape), pl.BlockSpec((1, POOL_WIDTH), lambda i: (0, 0)),
            _resident(w_ba.shape), _resident(w_bp.shape), _resident(w_out.shape),
            _resident(wgu.shape), _resident(wd.shape),
        ],
        out_specs=_row_spec(D_MODEL),
        out_shape=jax.ShapeDtypeStruct((n_tok, D_MODEL), F32),
        compiler_params=pltpu.CompilerParams(
            dimension_semantics=("parallel",), vmem_limit_bytes=VMEM_LIMIT_BYTES),
        name="merge",
    )(x, attn, u, u, u, sg, mod3, mod3, mod3, mod3, g3, g_final,
      w_pool, pool_scale, w_ba, w_bp, w_out, wgu, wd)


def _rope_tables(seq_len):
    lane = jnp.arange(V_DIM)
    within = lane % HEAD_DIM
    on_row_axis = within < ROPE_AXIS_DIM
    half = ROPE_AXIS_DIM // 2
    low_half = (within % ROPE_AXIS_DIM) < half
    freqs = ROPE_BASE ** (-jnp.arange(half, dtype=F32) / half)
    t = jnp.arange(seq_len)
    pos = jnp.where(on_row_axis[None, :], (t // GRID_W)[:, None], (t % GRID_W)[:, None]).astype(F32)
    ang = pos * freqs[within % half][None, :]
    cos, sin = jnp.cos(ang), jnp.sin(ang)
    zero = jnp.zeros_like(sin)
    return cos, jnp.where(low_half[None, :], -sin, zero), jnp.where(low_half[None, :], zero, sin)


def kernel(x, c, ctx, c_ctx, w_mod, b_mod, g_norm, w_ffn_gu, w_ffn_down, w_in, lambda_q1, lambda_k1,
           lambda_q2, lambda_k2, g_subln, w_pool, pool_scale, w_branch_attn, w_branch_pool, w_out,
           g_final):
    bsz, seq_len, _ = x.shape
    ctx_len = ctx.shape[1]
    assert w_mod.shape[0] == 1, "single-layer block"
    assert seq_len % TOKEN_TILE == 0 and (bsz * ctx_len) % TOKEN_TILE == 0 and bsz + 1 <= MOD_ROWS

    cc = jnp.concatenate([c, c_ctx[None, :], jnp.zeros((MOD_ROWS - bsz - 1, D_MODEL), F32)], axis=0)
    mod3 = _modulation(cc, w_mod[0], b_mod).reshape(MOD_ROWS, 1, N_MOD * D_MODEL)
    g3 = g_norm[0].reshape(3, 1, D_MODEL)

    wgu = w_ffn_gu[0].astype(BF16)
    wd = w_ffn_down[0].astype(BF16)
    w_in_b = w_in[0].astype(BF16)

    tiles_per_seq = seq_len // TOKEN_TILE
    lat_row = lambda i: i // tiles_per_seq
    ctx_row = lambda i: bsz

    lat = _ffn(x.reshape(bsz * seq_len, D_MODEL), mod3, lat_row, g3, 0, 0, wgu[0], wd[0])
    cx = _ffn(ctx.reshape(bsz * ctx_len, D_MODEL), mod3, ctx_row, g3, 0, 0, wgu[0], wd[0])

    q, k_l, v_l, u, sg = _mix_in_lat(lat, mod3, lat_row, g3, w_in_b, _rope_tables(seq_len), seq_len)
    k_c, v_c = _mix_in_ctx(cx, mod3, ctx_row, g3, w_in_b)

    lams = [v.reshape(1, HEAD_DIM) for v in (lambda_q1, lambda_k1, lambda_q2, lambda_k2)]
    attn = _attention(
        lams, g_subln.reshape(1, V_DIM),
        q.reshape(bsz, seq_len, QK_WIDTH),
        k_c.reshape(bsz, ctx_len, QK_WIDTH), k_l.reshape(bsz, seq_len, QK_WIDTH),
        v_c.reshape(bsz, ctx_len, ATTN_WIDTH), v_l.reshape(bsz, seq_len, ATTN_WIDTH))

    out = _merge(lat, attn.reshape(bsz * seq_len, ATTN_WIDTH), u, sg, mod3, lat_row, g3,
                 g_final.reshape(1, D_MODEL), w_pool[0].astype(BF16), pool_scale.reshape(1, POOL_WIDTH),
                 w_branch_attn[0].astype(BF16), w_branch_pool[0].astype(BF16), w_out[0].astype(BF16),
                 wgu[1], wd[1], seq_len)
    return out.reshape(bsz, seq_len, D_MODEL)
```

```python
import functools
import math

import jax
import jax.numpy as jnp
from jax import lax
from jax.experimental import pallas as pl
from jax.experimental.pallas import tpu as pltpu

F32 = jnp.float32
BF16 = jnp.bfloat16

D_MODEL = 1024
N_HEADS = 8
HEAD_DIM = 64
V_DIM = 2 * HEAD_DIM
QK_WIDTH = N_HEADS * 2 * HEAD_DIM
ATTN_WIDTH = N_HEADS * V_DIM
POOL_WINDOWS = (2, 4, 8, 16)
POOL_GROUP_DIM = 128
POOL_WIDTH = len(POOL_WINDOWS) * POOL_GROUP_DIM
POOL_HALO = 8
D_FF = 2816
GRID_W = 64
ROPE_BASE = 10000.0
ROPE_AXIS_DIM = HEAD_DIM // 2
N_MOD = 9
EPS = 1e-6
LAM_INIT = 0.8 - 0.6 * math.exp(-0.3 * 0)

Q_OFF = 0
K_OFF = Q_OFF + QK_WIDTH
V_OFF = K_OFF + QK_WIDTH
P_OFF = V_OFF + ATTN_WIDTH
G_OFF = P_OFF + POOL_WIDTH
IN_COLS = G_OFF + 2 * D_MODEL

LOG2E = 1.4426950408889634

VMEM_LIMIT_BYTES = 56 * 1024 * 1024
TOKEN_TILE = 512
FFN_TILE = 1024
MIX_TILE = 1024
CHAIN_ROWS = 512
FF_CHUNK = 256
Q_TILE = 256
ATTN_HEADS_PER_STEP = 4
MOD_ROWS = 24


def _sigmoid(x):
    return 1.0 / (1.0 + jnp.exp(-x))


def _rms_norm(x, g):
    return x * lax.rsqrt(jnp.mean(x * x, axis=-1, keepdims=True) + EPS) * g


def _mm(a, b):
    return jnp.dot(a, b, preferred_element_type=F32)


def _row_chains(tile):
    return [slice(r, r + CHAIN_ROWS) for r in range(0, tile, CHAIN_ROWS)]


def _swiglu(hs, wgu_ref, wd_ref):
    accs = [None] * len(hs)
    for j in range(D_FF // FF_CHUNK):
        lo = j * FF_CHUNK
        for c, h in enumerate(hs):
            a = _mm(h, wgu_ref[:, lo:lo + FF_CHUNK])
            b = _mm(h, wgu_ref[:, D_FF + lo:D_FF + lo + FF_CHUNK])
            t = (a * _sigmoid(a) * b).astype(BF16)
            part = _mm(t, wd_ref[lo:lo + FF_CHUNK, :])
            accs[c] = part if accs[c] is None else accs[c] + part
    return accs


def _load_bf16(src, dst, chunk_rows):
    n_rows, n_cols = dst.shape
    assert src.shape == dst.shape and n_rows % chunk_rows == 0
    n_chunks = n_rows // chunk_rows

    def body(stage, sem):
        def copy(c):
            return pltpu.make_async_copy(src.at[pl.ds(c * chunk_rows, chunk_rows), :],
                                         stage.at[c % 2], sem.at[c % 2])
        copy(0).start()
        for c in range(n_chunks):
            if c + 1 < n_chunks:
                copy(c + 1).start()
            copy(c).wait()
            dst[pl.ds(c * chunk_rows, chunk_rows), :] = stage[c % 2].astype(BF16)

    pl.run_scoped(body, pltpu.VMEM((2, chunk_rows, n_cols), F32), pltpu.SemaphoreType.DMA((2,)))


def _on_first_step(fn):
    pl.when(pl.program_id(0) == 0)(fn)


_HBM = pl.BlockSpec(memory_space=pl.ANY)
_SEQUENTIAL = pltpu.CompilerParams(dimension_semantics=("arbitrary",), vmem_limit_bytes=VMEM_LIMIT_BYTES)


def _mod_kernel(c_ref, w_ref, b_ref, o_ref):
    c = c_ref[...]
    s = (c * _sigmoid(c)).astype(BF16)
    o_ref[...] = _mm(s, w_ref[...].astype(BF16)) + b_ref[...]


def _modulation(cc, w_mod, b_mod):
    n_out = w_mod.shape[1]
    blk = D_MODEL
    return pl.pallas_call(
        _mod_kernel,
        grid=(n_out // blk,),
        in_specs=[
            pl.BlockSpec((MOD_ROWS, D_MODEL), lambda j: (0, 0)),
            pl.BlockSpec((D_MODEL, blk), lambda j: (0, j)),
            pl.BlockSpec((1, blk), lambda j: (0, j)),
        ],
        out_specs=pl.BlockSpec((MOD_ROWS, blk), lambda j: (0, j)),
        out_shape=jax.ShapeDtypeStruct((MOD_ROWS, n_out), F32),
        compiler_params=pltpu.CompilerParams(
            dimension_semantics=("arbitrary",), vmem_limit_bytes=VMEM_LIMIT_BYTES),
        name="mod",
    )(cc, w_mod, b_mod)


def _mod_spec(row_of_token, tile, chunk):
    return pl.BlockSpec((None, 1, D_MODEL), lambda i: (row_of_token(i * tile), 0, chunk))


def _row_spec(width, tile=TOKEN_TILE):
    return pl.BlockSpec((tile, width), lambda i: (i, 0))


def _vec_spec(idx):
    return pl.BlockSpec((None, 1, D_MODEL), lambda i: (idx, 0, 0))


def _load_ffn_weights(wgu_hbm, wd_hbm, half, wgu_ref, wd_ref):
    _load_bf16(wgu_hbm.at[0, half], wgu_ref, 128)
    _load_bf16(wd_hbm.at[0, half], wd_ref, 256)


def _ffn_kernel(x_ref, shift_ref, scale_ref, gate_ref, g_ref, wgu_hbm, wd_hbm, o_ref, wgu_ref, wd_ref,
                *, half):
    _on_first_step(lambda: _load_ffn_weights(wgu_hbm, wd_hbm, half, wgu_ref, wd_ref))
    chains = _row_chains(x_ref.shape[0])
    hs = [(_rms_norm(x_ref[rows, :], g_ref[...]) * (1.0 + scale_ref[...]) + shift_ref[...]).astype(BF16)
          for rows in chains]
    for rows, y in zip(chains, _swiglu(hs, wgu_ref, wd_ref)):
        o_ref[rows, :] = x_ref[rows, :] + (0.5 * gate_ref[...]) * y


def _ffn_weight_scratch():
    return [pltpu.VMEM((D_MODEL, 2 * D_FF), BF16), pltpu.VMEM((D_FF, D_MODEL), BF16)]


def _ffn(x, mod3, row_of_token, g3, g_idx, chunk0, w_ffn_gu, w_ffn_down, half):
    n_tok = x.shape[0]
    tile = FFN_TILE
    assert n_tok % tile == 0
    return pl.pallas_call(
        functools.partial(_ffn_kernel, half=half),
        grid=(n_tok // tile,),
        in_specs=[
            _row_spec(D_MODEL, tile),
            _mod_spec(row_of_token, tile, chunk0),
            _mod_spec(row_of_token, tile, chunk0 + 1),
            _mod_spec(row_of_token, tile, chunk0 + 2),
            _vec_spec(g_idx),
            _HBM, _HBM,
        ],
        out_specs=_row_spec(D_MODEL, tile),
        out_shape=jax.ShapeDtypeStruct((n_tok, D_MODEL), F32),
        scratch_shapes=_ffn_weight_scratch(),
        compiler_params=_SEQUENTIAL,
        name="ffn",
    )(x, mod3, mod3, mod3, g3, w_ffn_gu, w_ffn_down)


def _rope(z, cos, sin_lo, sin_hi):
    return z * cos + pltpu.roll(z, V_DIM - 16, 1) * sin_lo + pltpu.roll(z, 16, 1) * sin_hi


def _mix_in_lat_kernel(x_ref, shift_ref, scale_ref, g_ref, w_hbm, cos_ref, slo_ref, shi_ref,
                       q_ref, k_ref, v_ref, u_ref, sg_ref, w_ref):
    _on_first_step(lambda: _load_bf16(w_hbm.at[0], w_ref, 128))
    chains = _row_chains(x_ref.shape[0])
    hs = [(_rms_norm(x_ref[rows, :], g_ref[...]) * (1.0 + scale_ref[...]) + shift_ref[...]).astype(BF16)
          for rows in chains]
    q_scale = HEAD_DIM ** -0.5 * LOG2E

    def rope_store(z, rows, out_ref, scale):
        cos, slo, shi = cos_ref[rows, :], slo_ref[rows, :], shi_ref[rows, :]
        for hd in range(N_HEADS):
            sl = slice(hd * V_DIM, (hd + 1) * V_DIM)
            r = _rope(z[:, sl], cos, slo, shi)
            out_ref[hd, rows, :] = (r if scale is None else r * scale).astype(BF16)

    for rows, h in zip(chains, hs):
        rope_store(_mm(h, w_ref[:, Q_OFF:K_OFF]), rows, q_ref, q_scale)
    for rows, h in zip(chains, hs):
        rope_store(_mm(h, w_ref[:, K_OFF:V_OFF]), rows, k_ref, None)
    for rows, h in zip(chains, hs):
        v = _mm(h, w_ref[:, V_OFF:P_OFF]).astype(BF16)
        for hd in range(N_HEADS):
            v_ref[hd, rows, :] = v[:, hd * V_DIM:(hd + 1) * V_DIM]
    for rows, h in zip(chains, hs):
        u_ref[rows, :] = _mm(h, w_ref[:, P_OFF:G_OFF])
    for rows, h in zip(chains, hs):
        sg_ref[rows, :] = _sigmoid(_mm(h, w_ref[:, G_OFF:IN_COLS])).astype(BF16)


def _mix_in_lat(x, mod3, row_of_token, g3, w_in, tables, seq_len):
    n_tok = x.shape[0]
    tile = MIX_TILE
    tiles_per_seq = seq_len // tile
    tab_spec = pl.BlockSpec((tile, V_DIM), lambda i: (i % tiles_per_seq, 0))
    bf = lambda w: jax.ShapeDtypeStruct((n_tok, w), BF16)
    row = lambda w: _row_spec(w, tile)
    heads = pl.BlockSpec((None, N_HEADS, tile, V_DIM),
                         lambda i: (i // tiles_per_seq, 0, i % tiles_per_seq, 0))
    heads_shape = jax.ShapeDtypeStruct((n_tok // seq_len, N_HEADS, seq_len, V_DIM), BF16)
    return pl.pallas_call(
        _mix_in_lat_kernel,
        grid=(n_tok // tile,),
        in_specs=[
            row(D_MODEL),
            _mod_spec(row_of_token, tile, 3),
            _mod_spec(row_of_token, tile, 4),
            _vec_spec(1),
            _HBM,
            tab_spec, tab_spec, tab_spec,
        ],
        out_specs=[heads, heads, heads, row(POOL_WIDTH), row(2 * D_MODEL)],
        out_shape=[heads_shape, heads_shape, heads_shape,
                   jax.ShapeDtypeStruct((n_tok, POOL_WIDTH), F32), bf(2 * D_MODEL)],
        scratch_shapes=[pltpu.VMEM((D_MODEL, IN_COLS), BF16)],
        compiler_params=_SEQUENTIAL,
        name="mix_in_lat",
    )(x, mod3, mod3, g3, w_in, *tables)


def _mix_in_ctx_kernel(x_ref, shift_ref, scale_ref, g_ref, w_hbm, k_ref, v_ref, wkv_ref):
    _on_first_step(lambda: _load_bf16(w_hbm.at[0, :, pl.ds(K_OFF, P_OFF - K_OFF)], wkv_ref, 256))
    h = (_rms_norm(x_ref[...], g_ref[...]) * (1.0 + scale_ref[...]) + shift_ref[...]).astype(BF16)
    samples, _, ctx_len, _ = k_ref.shape
    for cols, out_ref in ((slice(0, QK_WIDTH), k_ref), (slice(QK_WIDTH, QK_WIDTH + ATTN_WIDTH), v_ref)):
        z = _mm(h, wkv_ref[:, cols]).astype(BF16)
        for s in range(samples):
            for hd in range(N_HEADS):
                out_ref[s, hd, :, :] = z[s * ctx_len:(s + 1) * ctx_len, hd * V_DIM:(hd + 1) * V_DIM]


def _mix_in_ctx(x, mod3, row_of_token, g3, w_in, ctx_len):
    n_tok = x.shape[0]
    assert V_OFF == K_OFF + QK_WIDTH and TOKEN_TILE % ctx_len == 0
    samples = TOKEN_TILE // ctx_len
    heads = pl.BlockSpec((samples, N_HEADS, ctx_len, V_DIM), lambda i: (i, 0, 0, 0))
    heads_shape = jax.ShapeDtypeStruct((n_tok // ctx_len, N_HEADS, ctx_len, V_DIM), BF16)
    return pl.pallas_call(
        _mix_in_ctx_kernel,
        grid=(n_tok // TOKEN_TILE,),
        in_specs=[
            _row_spec(D_MODEL),
            _mod_spec(row_of_token, TOKEN_TILE,3),
            _mod_spec(row_of_token, TOKEN_TILE,4),
            _vec_spec(1),
            _HBM,
        ],
        out_specs=[heads, heads],
        out_shape=[heads_shape, heads_shape],
        scratch_shapes=[pltpu.VMEM((D_MODEL, P_OFF - K_OFF), BF16)],
        compiler_params=_SEQUENTIAL,
        name="mix_in_ctx",
    )(x, mod3, mod3, g3, w_in)


def _attn_kernel(lq1_ref, lk1_ref, lq2_ref, lk2_ref, gs_ref, q_ref, kc_ref, kl_ref, vc_ref, vl_ref,
                 o_ref, k_scr, v_scr, sa_ref, sb_ref, ma_ref, mb_ref, oa_ref, ob_ref):
    lam = (jnp.exp(jnp.sum(lq1_ref[...] * lk1_ref[...], axis=-1, keepdims=True))
           - jnp.exp(jnp.sum(lq2_ref[...] * lk2_ref[...], axis=-1, keepdims=True))
           + LAM_INIT)
    gs = gs_ref[...] * (1.0 - LAM_INIT)
    n_heads, seq_len, _ = q_ref.shape
    ctx_len = kc_ref.shape[1]
    n_keys = k_scr.shape[1]
    for hd in range(n_heads):
        k_scr[hd, :ctx_len, :] = kc_ref[hd]
        k_scr[hd, ctx_len:, :] = kl_ref[hd]
        v_scr[hd, :ctx_len, :V_DIM] = vc_ref[hd]
        v_scr[hd, ctx_len:, :V_DIM] = vl_ref[hd]
        v_scr[hd, :, V_DIM:] = jnp.ones((n_keys, V_DIM), BF16)
    first_comp = lax.broadcasted_iota(jnp.int32, (Q_TILE, V_DIM), 1) < HEAD_DIM
    nt = (((1,), (1,)), ((), ()))
    blocks_per_head = seq_len // Q_TILE
    assert blocks_per_head & (blocks_per_head - 1) == 0
    shift = blocks_per_head.bit_length() - 1

    def head_rows(t):
        if isinstance(t, int):
            return t // blocks_per_head, pl.ds((t % blocks_per_head) * Q_TILE, Q_TILE)
        hd = lax.shift_right_logical(t, shift)
        return hd, pl.ds(pl.multiple_of((t - (hd << shift)) * Q_TILE, Q_TILE), Q_TILE)

    def scores(t, s_ref, m_ref):
        hd, rows = head_rows(t)
        q = q_ref[hd, rows, :]
        zero = jnp.zeros_like(q)
        qq = jnp.concatenate([jnp.where(first_comp, q, zero), jnp.where(first_comp, zero, q)], axis=0)
        s = lax.dot_general(qq, k_scr[hd], nt, preferred_element_type=F32)
        s_ref[...] = s
        m_ref[...] = jnp.max(s, axis=-1, keepdims=True)

    def values(t, s_ref, m_ref, ov_ref):
        hd, _ = head_rows(t)
        for comp in range(2):
            r = slice(comp * Q_TILE, (comp + 1) * Q_TILE)
            p = jnp.exp2(s_ref[r, :] - m_ref[r, :]).astype(BF16)
            ov_ref[r, :] = _mm(p, v_scr[hd])

    def finish(t, ov_ref):
        hd, rows = head_rows(t)
        on = ov_ref[:, :V_DIM] / ov_ref[:, V_DIM:]
        o = on[:Q_TILE] - lam * on[Q_TILE:]
        o_ref[hd, rows, :] = _rms_norm(o, gs).astype(BF16)

    n_blocks = n_heads * blocks_per_head
    assert n_blocks % 2 == 0 and n_blocks >= 4
    scores(0, sa_ref, ma_ref)
    scores(1, sb_ref, mb_ref)
    values(0, sa_ref, ma_ref, oa_ref)

    def pair(j, carry):
        scores(2 * j + 2, sa_ref, ma_ref)
        values(2 * j + 1, sb_ref, mb_ref, ob_ref)
        finish(2 * j, oa_ref)
        scores(2 * j + 3, sb_ref, mb_ref)
        values(2 * j + 2, sa_ref, ma_ref, oa_ref)
        finish(2 * j + 1, ob_ref)
        return carry

    lax.fori_loop(0, n_blocks // 2 - 1, pair, 0)
    values(n_blocks - 1, sb_ref, mb_ref, ob_ref)
    finish(n_blocks - 2, oa_ref)
    finish(n_blocks - 1, ob_ref)


def _attention(lams, gs, q, kc, kl, vc, vl):
    bsz, _, seq_len, _ = q.shape
    ctx_len = kc.shape[2]
    n_keys = ctx_len + seq_len
    hg = ATTN_HEADS_PER_STEP
    lam_spec = pl.BlockSpec((1, HEAD_DIM), lambda b, g: (0, 0))
    lat_spec = pl.BlockSpec((None, hg, seq_len, V_DIM), lambda b, g: (b, g, 0, 0))
    ctx_spec = pl.BlockSpec((None, hg, ctx_len, V_DIM), lambda b, g: (b, g, 0, 0))
    return pl.pallas_call(
        _attn_kernel,
        grid=(bsz, N_HEADS // hg),
        in_specs=[lam_spec, lam_spec, lam_spec, lam_spec,
                  pl.BlockSpec((1, V_DIM), lambda b, g: (0, 0)),
                  lat_spec, ctx_spec, lat_spec, ctx_spec, lat_spec],
        out_specs=lat_spec,
        out_shape=jax.ShapeDtypeStruct((bsz, N_HEADS, seq_len, V_DIM), BF16),
        scratch_shapes=[pltpu.VMEM((hg, n_keys, V_DIM), BF16),
                        pltpu.VMEM((hg, n_keys, 2 * V_DIM), BF16),
                        pltpu.VMEM((2 * Q_TILE, n_keys), F32), pltpu.VMEM((2 * Q_TILE, n_keys), F32),
                        pltpu.VMEM((2 * Q_TILE, 1), F32), pltpu.VMEM((2 * Q_TILE, 1), F32),
                        pltpu.VMEM((2 * Q_TILE, 2 * V_DIM), F32), pltpu.VMEM((2 * Q_TILE, 2 * V_DIM), F32)],
        compiler_params=pltpu.CompilerParams(
            dimension_semantics=("parallel", "parallel"), vmem_limit_bytes=VMEM_LIMIT_BYTES),
        name="attn",
    )(*lams, gs, q, kc, kl, vc, vl)


def _pool_branch(u, u_prev, u_next, wpool_ref, pscale_ref, tiles_per_seq, seq_len):
    i = pl.program_id(0)
    pos0 = (i % tiles_per_seq) * TOKEN_TILE
    u_prev = jnp.where(pos0 > 0, u_prev, 0.0)
    u_next = jnp.where(pos0 + TOKEN_TILE < seq_len, u_next, 0.0)
    ext = jnp.concatenate([u_prev, u, u_next], axis=0)
    n_ext = ext.shape[0]
    pos = pos0 + lax.broadcasted_iota(jnp.int32, (TOKEN_TILE, POOL_GROUP_DIM), 0)
    outs = []
    for g, w in enumerate(POOL_WINDOWS):
        cols = slice(g * POOL_GROUP_DIM, (g + 1) * POOL_GROUP_DIM)
        win = ext[:, cols]
        win = win + pltpu.roll(win, 1, 0)
        span = 1
        while 2 * span < w:
            win = pltpu.roll(win, span, 0) + pltpu.roll(win, n_ext - span, 0)
            span *= 2
        win = win[POOL_HALO:POOL_HALO + TOKEN_TILE]
        cnt = (jnp.minimum(pos + w // 2, seq_len) - jnp.maximum(pos - w // 2, 0)).astype(F32)
        pooled = (win / cnt - u[:, cols]).astype(BF16)
        outs.append(_mm(pooled, wpool_ref[cols, :]))
    return jnp.concatenate(outs, axis=-1) * pscale_ref[...]


def _merge_kernel(x_ref, attn_ref, u_ref, up_ref, un_ref, sg_ref,
                  gate_mix_ref, shift_ref, scale_ref, gate_ffn_ref, g_ref, gfin_ref, pscale_ref,
                  wpool_hbm, wba_hbm, wbp_hbm, wout_hbm, wgu_hbm, wd_hbm,
                  o_ref, wpool_ref, wba_ref, wbp_ref, wout_ref, wgu_ref, wd_ref, *, tiles_per_seq, seq_len):
    def load_weights():
        _load_bf16(wpool_hbm.at[0], wpool_ref, 256)
        _load_bf16(wba_hbm.at[0], wba_ref, 256)
        _load_bf16(wbp_hbm.at[0], wbp_ref, 256)
        _load_bf16(wout_hbm.at[0], wout_ref, 256)
        _load_ffn_weights(wgu_hbm, wd_hbm, 1, wgu_ref, wd_ref)

    _on_first_step(load_weights)
    attn = jnp.concatenate([attn_ref[hd] for hd in range(N_HEADS)], axis=-1)
    ya = sg_ref[:, :D_MODEL].astype(F32) * _mm(attn, wba_ref[...])
    pool = _pool_branch(u_ref[...], up_ref[...], un_ref[...], wpool_ref, pscale_ref,
                        tiles_per_seq, seq_len)
    y = ya + sg_ref[:, D_MODEL:].astype(F32) * _mm(pool.astype(BF16), wbp_ref[...])
    x = x_ref[...] + gate_mix_ref[...] * _mm(y.astype(BF16), wout_ref[...])
    h = (_rms_norm(x, g_ref[...]) * (1.0 + scale_ref[...]) + shift_ref[...]).astype(BF16)
    x = x + (0.5 * gate_ffn_ref[...]) * _swiglu([h], wgu_ref, wd_ref)[0]
    o_ref[...] = _rms_norm(x, gfin_ref[...])


def _merge(x, attn, u, sg, mod3, row_of_token, g3, g_final, pool_scale, w_pool, w_ba, w_bp, w_out,
           w_ffn_gu, w_ffn_down, seq_len):
    n_tok = x.shape[0]
    tiles_per_seq = seq_len // TOKEN_TILE
    halo_blocks = TOKEN_TILE // POOL_HALO
    n_halo = n_tok // POOL_HALO
    prev_spec = pl.BlockSpec((POOL_HALO, POOL_WIDTH),
                             lambda i: (jnp.maximum(i * halo_blocks - 1, 0), 0))
    next_spec = pl.BlockSpec((POOL_HALO, POOL_WIDTH),
                             lambda i: (jnp.minimum((i + 1) * halo_blocks, n_halo - 1), 0))
    kern = functools.partial(_merge_kernel, tiles_per_seq=tiles_per_seq, seq_len=seq_len)
    return pl.pallas_call(
        kern,
        grid=(n_tok // TOKEN_TILE,),
        in_specs=[
            _row_spec(D_MODEL),
            pl.BlockSpec((None, N_HEADS, TOKEN_TILE, V_DIM),
                         lambda i: (i // tiles_per_seq, 0, i % tiles_per_seq, 0)),
            _row_spec(POOL_WIDTH), prev_spec, next_spec,
            _row_spec(2 * D_MODEL),
            _mod_spec(row_of_token, TOKEN_TILE, 5), _mod_spec(row_of_token, TOKEN_TILE, 6),
            _mod_spec(row_of_token, TOKEN_TILE, 7), _mod_spec(row_of_token, TOKEN_TILE, 8), _vec_spec(2),
            pl.BlockSpec((1, D_MODEL), lambda i: (0, 0)),
            pl.BlockSpec((1, POOL_WIDTH), lambda i: (0, 0)),
            _HBM, _HBM, _HBM, _HBM, _HBM, _HBM,
        ],
        out_specs=_row_spec(D_MODEL),
        out_shape=jax.ShapeDtypeStruct((n_tok, D_MODEL), F32),
        scratch_shapes=[pltpu.VMEM((POOL_WIDTH, POOL_GROUP_DIM), BF16),
                        pltpu.VMEM((ATTN_WIDTH, D_MODEL), BF16), pltpu.VMEM((POOL_WIDTH, D_MODEL), BF16),
                        pltpu.VMEM((D_MODEL, D_MODEL), BF16)] + _ffn_weight_scratch(),
        compiler_params=_SEQUENTIAL,
        name="merge",
    )(x, attn, u, u, u, sg, mod3, mod3, mod3, mod3, g3, g_final, pool_scale,
      w_pool, w_ba, w_bp, w_out, w_ffn_gu, w_ffn_down)


def _rope_tables(seq_len):
    lane = jnp.arange(V_DIM)
    within = lane % HEAD_DIM
    on_row_axis = within < ROPE_AXIS_DIM
    half = ROPE_AXIS_DIM // 2
    low_half = (within % ROPE_AXIS_DIM) < half
    freqs = ROPE_BASE ** (-jnp.arange(half, dtype=F32) / half)
    t = jnp.arange(seq_len)
    pos = jnp.where(on_row_axis[None, :], (t // GRID_W)[:, None], (t % GRID_W)[:, None]).astype(F32)
    ang = pos * freqs[within % half][None, :]
    cos, sin = jnp.cos(ang), jnp.sin(ang)
    zero = jnp.zeros_like(sin)
    return cos, jnp.where(low_half[None, :], -sin, zero), jnp.where(low_half[None, :], zero, sin)


def kernel(x, c, ctx, c_ctx, w_mod, b_mod, g_norm, w_ffn_gu, w_ffn_down, w_in, lambda_q1, lambda_k1,
           lambda_q2, lambda_k2, g_subln, w_pool, pool_scale, w_branch_attn, w_branch_pool, w_out,
           g_final):
    bsz, seq_len, _ = x.shape
    ctx_len = ctx.shape[1]
    assert w_mod.shape[0] == 1, "single-layer block"
    assert seq_len % max(TOKEN_TILE, FFN_TILE) == 0 and bsz + 1 <= MOD_ROWS

    cc = jnp.concatenate([c, c_ctx[None, :], jnp.zeros((MOD_ROWS - bsz - 1, D_MODEL), F32)], axis=0)
    mod3 = _modulation(cc, w_mod[0], b_mod).reshape(MOD_ROWS, 1, N_MOD * D_MODEL)
    g3 = g_norm[0].reshape(3, 1, D_MODEL)

    lat_row = lambda first_token: first_token // seq_len
    ctx_row = lambda first_token: bsz

    lat = _ffn(x.reshape(bsz * seq_len, D_MODEL), mod3, lat_row, g3, 0, 0, w_ffn_gu, w_ffn_down, 0)
    cx = _ffn(ctx.reshape(bsz * ctx_len, D_MODEL), mod3, ctx_row, g3, 0, 0, w_ffn_gu, w_ffn_down, 0)

    q, k_l, v_l, u, sg = _mix_in_lat(lat, mod3, lat_row, g3, w_in, _rope_tables(seq_len), seq_len)
    k_c, v_c = _mix_in_ctx(cx, mod3, ctx_row, g3, w_in, ctx_len)

    lams = [v.reshape(1, HEAD_DIM) for v in (lambda_q1, lambda_k1, lambda_q2, lambda_k2)]
    attn = _attention(lams, g_subln.reshape(1, V_DIM), q, k_c, k_l, v_c, v_l)

    out = _merge(lat, attn, u, sg, mod3, lat_row, g3,
                 g_final.reshape(1, D_MODEL), pool_scale.reshape(1, POOL_WIDTH),
                 w_pool.reshape(1, POOL_WIDTH, POOL_GROUP_DIM), w_branch_attn, w_branch_pool, w_out,
                 w_ffn_gu, w_ffn_down, seq_len)
    return out.reshape(bsz, seq_len, D_MODEL)
```

```python
import functools
import math

import jax
import jax.numpy as jnp
from jax import lax
from jax.experimental import pallas as pl
from jax.experimental.pallas import tpu as pltpu

F32 = jnp.float32
BF16 = jnp.bfloat16

D_MODEL = 1024
N_HEADS = 8
HEAD_DIM = 64
V_DIM = 2 * HEAD_DIM
QK_WIDTH = N_HEADS * 2 * HEAD_DIM
ATTN_WIDTH = N_HEADS * V_DIM
POOL_WINDOWS = (2, 4, 8, 16)
POOL_GROUP_DIM = 128
POOL_WIDTH = len(POOL_WINDOWS) * POOL_GROUP_DIM
POOL_HALO = 8
D_FF = 2816
GRID_W = 64
ROPE_BASE = 10000.0
ROPE_AXIS_DIM = HEAD_DIM // 2
N_MOD = 9
EPS = 1e-6
LAM_INIT = 0.8 - 0.6 * math.exp(-0.3 * 0)

Q_OFF = 0
K_OFF = Q_OFF + QK_WIDTH
V_OFF = K_OFF + QK_WIDTH
P_OFF = V_OFF + ATTN_WIDTH
G_OFF = P_OFF + POOL_WIDTH
IN_COLS = G_OFF + 2 * D_MODEL

LOG2E = 1.4426950408889634

VMEM_LIMIT_BYTES = 56 * 1024 * 1024
TOKEN_TILE = 512
FFN_TILE = 1024
MIX_TILE = 1024
CHAIN_ROWS = 512
FF_CHUNK = 256
Q_TILE = 512
ATTN_HEADS_PER_STEP = 4
MOD_ROWS = 24


def _sigmoid(x):
    return 1.0 / (1.0 + jnp.exp(-x))


def _rms_norm(x, g):
    return x * lax.rsqrt(jnp.mean(x * x, axis=-1, keepdims=True) + EPS) * g


def _ada_norm(x, g_ref, scale_ref, shift_ref):
    w = g_ref[...] * (1.0 + scale_ref[...])
    inv = lax.rsqrt(jnp.mean(x * x, axis=-1, keepdims=True) + EPS)
    return (x * inv * w + shift_ref[...]).astype(BF16)


def _mm(a, b):
    return jnp.dot(a, b, preferred_element_type=F32)


def _row_chains(tile):
    return [slice(r, r + CHAIN_ROWS) for r in range(0, tile, CHAIN_ROWS)]


def _swiglu_chunks(hs, accs, chunk_ids, wgu_ref, wd_ref):
    accs = list(accs)
    for j in chunk_ids:
        lo = j * FF_CHUNK
        for c, h in enumerate(hs):
            a = _mm(h, wgu_ref[:, lo:lo + FF_CHUNK])
            b = _mm(h, wgu_ref[:, D_FF + lo:D_FF + lo + FF_CHUNK])
            t = (a * _sigmoid(a) * b).astype(BF16)
            part = _mm(t, wd_ref[lo:lo + FF_CHUNK, :])
            accs[c] = part if accs[c] is None else accs[c] + part
    return accs


def _swiglu(hs, wgu_ref, wd_ref):
    return _swiglu_chunks(hs, [None] * len(hs), range(D_FF // FF_CHUNK), wgu_ref, wd_ref)


def _load_bf16(src, dst, chunk_rows):
    n_rows, n_cols = dst.shape
    assert src.shape == dst.shape and n_rows % chunk_rows == 0
    n_chunks = n_rows // chunk_rows

    def body(stage, sem):
        def copy(c):
            return pltpu.make_async_copy(src.at[pl.ds(c * chunk_rows, chunk_rows), :],
                                         stage.at[c % 2], sem.at[c % 2])
        copy(0).start()
        for c in range(n_chunks):
            if c + 1 < n_chunks:
                copy(c + 1).start()
            copy(c).wait()
            dst[pl.ds(c * chunk_rows, chunk_rows), :] = stage[c % 2].astype(BF16)

    pl.run_scoped(body, pltpu.VMEM((2, chunk_rows, n_cols), F32), pltpu.SemaphoreType.DMA((2,)))


def _on_first_step(fn):
    pl.when(pl.program_id(0) == 0)(fn)


_HBM = pl.BlockSpec(memory_space=pl.ANY)
_SEQUENTIAL = pltpu.CompilerParams(dimension_semantics=("arbitrary",), vmem_limit_bytes=VMEM_LIMIT_BYTES)


def _mod_kernel(c_ref, w_ref, b_ref, o_ref):
    c = c_ref[...]
    s = (c * _sigmoid(c)).astype(BF16)
    o_ref[...] = _mm(s, w_ref[...].astype(BF16)) + b_ref[...]


def _modulation(cc, w_mod, b_mod):
    n_out = w_mod.shape[1]
    blk = D_MODEL
    return pl.pallas_call(
        _mod_kernel,
        grid=(n_out // blk,),
        in_specs=[
            pl.BlockSpec((MOD_ROWS, D_MODEL), lambda j: (0, 0)),
            pl.BlockSpec((D_MODEL, blk), lambda j: (0, j)),
            pl.BlockSpec((1, blk), lambda j: (0, j)),
        ],
        out_specs=pl.BlockSpec((MOD_ROWS, blk), lambda j: (0, j)),
        out_shape=jax.ShapeDtypeStruct((MOD_ROWS, n_out), F32),
        compiler_params=pltpu.CompilerParams(
            dimension_semantics=("arbitrary",), vmem_limit_bytes=VMEM_LIMIT_BYTES),
        name="mod",
    )(cc, w_mod, b_mod)


def _mod_spec(row_of_token, tile, chunk):
    return pl.BlockSpec((None, 1, D_MODEL), lambda i: (row_of_token(i * tile), 0, chunk))


def _row_spec(width, tile=TOKEN_TILE):
    return pl.BlockSpec((tile, width), lambda i: (i, 0))


def _vec_spec(idx):
    return pl.BlockSpec((None, 1, D_MODEL), lambda i: (idx, 0, 0))


def _load_ffn_weights(wgu_hbm, wd_hbm, half, wgu_ref, wd_ref):
    _load_bf16(wgu_hbm.at[0, half], wgu_ref, 128)
    _load_bf16(wd_hbm.at[0, half], wd_ref, 256)


def _ffn_kernel(x_ref, shift_ref, scale_ref, gate_ref, g_ref, wgu_hbm, wd_hbm, o_ref, wgu_ref, wd_ref,
                *, half):
    _on_first_step(lambda: _load_ffn_weights(wgu_hbm, wd_hbm, half, wgu_ref, wd_ref))
    chains = _row_chains(x_ref.shape[0])
    hs = [_ada_norm(x_ref[rows, :], g_ref, scale_ref, shift_ref)
          for rows in chains]
    for rows, y in zip(chains, _swiglu(hs, wgu_ref, wd_ref)):
        o_ref[rows, :] = x_ref[rows, :] + (0.5 * gate_ref[...]) * y


def _ffn_weight_scratch():
    return [pltpu.VMEM((D_MODEL, 2 * D_FF), BF16), pltpu.VMEM((D_FF, D_MODEL), BF16)]


def _ffn(x, mod3, row_of_token, g3, g_idx, chunk0, w_ffn_gu, w_ffn_down, half):
    n_tok = x.shape[0]
    tile = FFN_TILE
    assert n_tok % tile == 0
    return pl.pallas_call(
        functools.partial(_ffn_kernel, half=half),
        grid=(n_tok // tile,),
        in_specs=[
            _row_spec(D_MODEL, tile),
            _mod_spec(row_of_token, tile, chunk0),
            _mod_spec(row_of_token, tile, chunk0 + 1),
            _mod_spec(row_of_token, tile, chunk0 + 2),
            _vec_spec(g_idx),
            _HBM, _HBM,
        ],
        out_specs=_row_spec(D_MODEL, tile),
        out_shape=jax.ShapeDtypeStruct((n_tok, D_MODEL), F32),
        scratch_shapes=_ffn_weight_scratch(),
        compiler_params=_SEQUENTIAL,
        name="ffn",
    )(x, mod3, mod3, mod3, g3, w_ffn_gu, w_ffn_down)


def _rope(z, cos, sin_lo, sin_hi):
    return z * cos + pltpu.roll(z, V_DIM - 16, 1) * sin_lo + pltpu.roll(z, 16, 1) * sin_hi


def _mix_in_lat_kernel(x_ref, shift_ref, scale_ref, g_ref, w_hbm, cos_ref, slo_ref, shi_ref,
                       q_ref, k_ref, v_ref, u_ref, sg_ref, w_ref):
    _on_first_step(lambda: _load_bf16(w_hbm.at[0], w_ref, 128))
    chains = _row_chains(x_ref.shape[0])
    hs = [_ada_norm(x_ref[rows, :], g_ref, scale_ref, shift_ref)
          for rows in chains]
    q_scale = HEAD_DIM ** -0.5 * LOG2E

    def rope_store(z, rows, out_ref, scale):
        cos, slo, shi = cos_ref[rows, :], slo_ref[rows, :], shi_ref[rows, :]
        for hd in range(N_HEADS):
            sl = slice(hd * V_DIM, (hd + 1) * V_DIM)
            r = _rope(z[:, sl], cos, slo, shi)
            out_ref[hd, rows, :] = (r if scale is None else r * scale).astype(BF16)

    for rows, h in zip(chains, hs):
        rope_store(_mm(h, w_ref[:, Q_OFF:K_OFF]), rows, q_ref, q_scale)
    for rows, h in zip(chains, hs):
        rope_store(_mm(h, w_ref[:, K_OFF:V_OFF]), rows, k_ref, None)
    for rows, h in zip(chains, hs):
        v = _mm(h, w_ref[:, V_OFF:P_OFF]).astype(BF16)
        for hd in range(N_HEADS):
            v_ref[hd, rows, :] = v[:, hd * V_DIM:(hd + 1) * V_DIM]
    for rows, h in zip(chains, hs):
        u_ref[rows, :] = _mm(h, w_ref[:, P_OFF:G_OFF])
    for rows, h in zip(chains, hs):
        sg_ref[rows, :] = _sigmoid(_mm(h, w_ref[:, G_OFF:IN_COLS])).astype(BF16)


def _mix_in_lat(x, mod3, row_of_token, g3, w_in, tables, seq_len):
    n_tok = x.shape[0]
    tile = MIX_TILE
    tiles_per_seq = seq_len // tile
    tab_spec = pl.BlockSpec((tile, V_DIM), lambda i: (i % tiles_per_seq, 0))
    bf = lambda w: jax.ShapeDtypeStruct((n_tok, w), BF16)
    row = lambda w: _row_spec(w, tile)
    heads = pl.BlockSpec((None, N_HEADS, tile, V_DIM),
                         lambda i: (i // tiles_per_seq, 0, i % tiles_per_seq, 0))
    heads_shape = jax.ShapeDtypeStruct((n_tok // seq_len, N_HEADS, seq_len, V_DIM), BF16)
    return pl.pallas_call(
        _mix_in_lat_kernel,
        grid=(n_tok // tile,),
        in_specs=[
            row(D_MODEL),
            _mod_spec(row_of_token, tile, 3),
            _mod_spec(row_of_token, tile, 4),
            _vec_spec(1),
            _HBM,
            tab_spec, tab_spec, tab_spec,
        ],
        out_specs=[heads, heads, heads, row(POOL_WIDTH), row(2 * D_MODEL)],
        out_shape=[heads_shape, heads_shape, heads_shape,
                   jax.ShapeDtypeStruct((n_tok, POOL_WIDTH), F32), bf(2 * D_MODEL)],
        scratch_shapes=[pltpu.VMEM((D_MODEL, IN_COLS), BF16)],
        compiler_params=_SEQUENTIAL,
        name="mix_in_lat",
    )(x, mod3, mod3, g3, w_in, *tables)


def _mix_in_ctx_kernel(x_ref, shift_ref, scale_ref, g_ref, w_hbm, k_ref, v_ref, wkv_ref):
    _on_first_step(lambda: _load_bf16(w_hbm.at[0, :, pl.ds(K_OFF, P_OFF - K_OFF)], wkv_ref, 256))
    h = _ada_norm(x_ref[...], g_ref, scale_ref, shift_ref)
    samples, _, ctx_len, _ = k_ref.shape
    for cols, out_ref in ((slice(0, QK_WIDTH), k_ref), (slice(QK_WIDTH, QK_WIDTH + ATTN_WIDTH), v_ref)):
        z = _mm(h, wkv_ref[:, cols]).astype(BF16)
        for s in range(samples):
            for hd in range(N_HEADS):
                out_ref[s, hd, :, :] = z[s * ctx_len:(s + 1) * ctx_len, hd * V_DIM:(hd + 1) * V_DIM]


def _mix_in_ctx(x, mod3, row_of_token, g3, w_in, ctx_len):
    n_tok = x.shape[0]
    assert V_OFF == K_OFF + QK_WIDTH and TOKEN_TILE % ctx_len == 0
    samples = TOKEN_TILE // ctx_len
    heads = pl.BlockSpec((samples, N_HEADS, ctx_len, V_DIM), lambda i: (i, 0, 0, 0))
    heads_shape = jax.ShapeDtypeStruct((n_tok // ctx_len, N_HEADS, ctx_len, V_DIM), BF16)
    return pl.pallas_call(
        _mix_in_ctx_kernel,
        grid=(n_tok // TOKEN_TILE,),
        in_specs=[
            _row_spec(D_MODEL),
            _mod_spec(row_of_token, TOKEN_TILE,3),
            _mod_spec(row_of_token, TOKEN_TILE,4),
            _vec_spec(1),
            _HBM,
        ],
        out_specs=[heads, heads],
        out_shape=[heads_shape, heads_shape],
        scratch_shapes=[pltpu.VMEM((D_MODEL, P_OFF - K_OFF), BF16)],
        compiler_params=_SEQUENTIAL,
        name="mix_in_ctx",
    )(x, mod3, mod3, g3, w_in)


def _attn_kernel(lq1_ref, lk1_ref, lq2_ref, lk2_ref, gs_ref, q_ref, kc_ref, kl_ref, vc_ref, vl_ref,
                 o_ref, k_scr, v_scr, sa_ref, sb_ref, ma_ref, mb_ref, oa_ref, ob_ref):
    lam = (jnp.exp(jnp.sum(lq1_ref[...] * lk1_ref[...], axis=-1, keepdims=True))
           - jnp.exp(jnp.sum(lq2_ref[...] * lk2_ref[...], axis=-1, keepdims=True))
           + LAM_INIT)
    gs = gs_ref[...] * (1.0 - LAM_INIT)
    n_heads, seq_len, _ = q_ref.shape
    ctx_len = kc_ref.shape[1]
    n_keys = k_scr.shape[1]
    for hd in range(n_heads):
        k_scr[hd, :ctx_len, :] = kc_ref[hd]
        k_scr[hd, ctx_len:, :] = kl_ref[hd]
        v_scr[hd, :ctx_len, :V_DIM] = vc_ref[hd]
        v_scr[hd, ctx_len:, :V_DIM] = vl_ref[hd]
        v_scr[hd, :, V_DIM:] = jnp.ones((n_keys, V_DIM), BF16)
    first_comp = lax.broadcasted_iota(jnp.int32, (Q_TILE, V_DIM), 1) < HEAD_DIM
    nt = (((1,), (1,)), ((), ()))
    blocks_per_head = seq_len // Q_TILE
    assert blocks_per_head & (blocks_per_head - 1) == 0
    shift = blocks_per_head.bit_length() - 1

    def head_rows(t):
        if isinstance(t, int):
            return t // blocks_per_head, pl.ds((t % blocks_per_head) * Q_TILE, Q_TILE)
        hd = lax.shift_right_logical(t, shift)
        return hd, pl.ds(pl.multiple_of((t - (hd << shift)) * Q_TILE, Q_TILE), Q_TILE)

    def scores(t, s_ref, m_ref):
        hd, rows = head_rows(t)
        q = q_ref[hd, rows, :]
        zero = jnp.zeros_like(q)
        qq = jnp.concatenate([jnp.where(first_comp, q, zero), jnp.where(first_comp, zero, q)], axis=0)
        s = lax.dot_general(qq, k_scr[hd], nt, preferred_element_type=F32)
        s_ref[...] = s
        m_ref[...] = jnp.max(s, axis=-1, keepdims=True)

    def values(t, s_ref, m_ref, ov_ref):
        hd, _ = head_rows(t)
        for comp in range(2):
            r = slice(comp * Q_TILE, (comp + 1) * Q_TILE)
            p = jnp.exp2(s_ref[r, :] - m_ref[r, :]).astype(BF16)
            ov_ref[r, :] = _mm(p, v_scr[hd])

    def finish(t, ov_ref):
        hd, rows = head_rows(t)
        on = ov_ref[:, :V_DIM] / ov_ref[:, V_DIM:]
        o = on[:Q_TILE] - lam * on[Q_TILE:]
        o_ref[hd, rows, :] = _rms_norm(o, gs).astype(BF16)

    n_blocks = n_heads * blocks_per_head
    assert n_blocks % 2 == 0 and n_blocks >= 4
    scores(0, sa_ref, ma_ref)
    scores(1, sb_ref, mb_ref)
    values(0, sa_ref, ma_ref, oa_ref)

    def pair(j, carry):
        scores(2 * j + 2, sa_ref, ma_ref)
        values(2 * j + 1, sb_ref, mb_ref, ob_ref)
        finish(2 * j, oa_ref)
        scores(2 * j + 3, sb_ref, mb_ref)
        values(2 * j + 2, sa_ref, ma_ref, oa_ref)
        finish(2 * j + 1, ob_ref)
        return carry

    lax.fori_loop(0, n_blocks // 2 - 1, pair, 0)
    values(n_blocks - 1, sb_ref, mb_ref, ob_ref)
    finish(n_blocks - 2, oa_ref)
    finish(n_blocks - 1, ob_ref)


def _attention(lams, gs, q, kc, kl, vc, vl):
    bsz, _, seq_len, _ = q.shape
    ctx_len = kc.shape[2]
    n_keys = ctx_len + seq_len
    hg = ATTN_HEADS_PER_STEP
    lam_spec = pl.BlockSpec((1, HEAD_DIM), lambda b, g: (0, 0))
    lat_spec = pl.BlockSpec((None, hg, seq_len, V_DIM), lambda b, g: (b, g, 0, 0))
    ctx_spec = pl.BlockSpec((None, hg, ctx_len, V_DIM), lambda b, g: (b, g, 0, 0))
    return pl.pallas_call(
        _attn_kernel,
        grid=(bsz, N_HEADS // hg),
        in_specs=[lam_spec, lam_spec, lam_spec, lam_spec,
                  pl.BlockSpec((1, V_DIM), lambda b, g: (0, 0)),
                  lat_spec, ctx_spec, lat_spec, ctx_spec, lat_spec],
        out_specs=lat_spec,
        out_shape=jax.ShapeDtypeStruct((bsz, N_HEADS, seq_len, V_DIM), BF16),
        scratch_shapes=[pltpu.VMEM((hg, n_keys, V_DIM), BF16),
                        pltpu.VMEM((hg, n_keys, 2 * V_DIM), BF16),
                        pltpu.VMEM((2 * Q_TILE, n_keys), F32), pltpu.VMEM((2 * Q_TILE, n_keys), F32),
                        pltpu.VMEM((2 * Q_TILE, 1), F32), pltpu.VMEM((2 * Q_TILE, 1), F32),
                        pltpu.VMEM((2 * Q_TILE, 2 * V_DIM), F32), pltpu.VMEM((2 * Q_TILE, 2 * V_DIM), F32)],
        compiler_params=pltpu.CompilerParams(
            dimension_semantics=("parallel", "parallel"), vmem_limit_bytes=VMEM_LIMIT_BYTES),
        name="attn",
    )(*lams, gs, q, kc, kl, vc, vl)


def _pool_branch(u, u_prev, u_next, wpool_ref, pscale_ref, pos0, seq_len):
    u_prev = jnp.where(pos0 > 0, u_prev, 0.0)
    u_next = jnp.where(pos0 + TOKEN_TILE < seq_len, u_next, 0.0)
    ext = jnp.concatenate([u_prev, u, u_next], axis=0)
    n_ext = ext.shape[0]
    edge = lax.broadcasted_iota(jnp.int32, (POOL_HALO, POOL_GROUP_DIM), 0)
    edge_pos = (pos0 + edge, pos0 + (TOKEN_TILE - POOL_HALO) + edge)
    outs = []
    for g, w in enumerate(POOL_WINDOWS):
        cols = slice(g * POOL_GROUP_DIM, (g + 1) * POOL_GROUP_DIM)
        fwd = ext[:, cols]
        span = 1
        while 2 * span < w:
            fwd = fwd + pltpu.roll(fwd, n_ext - span, 0)
            span *= 2
        win = (fwd + pltpu.roll(fwd, span, 0))[POOL_HALO:POOL_HALO + TOKEN_TILE]
        clipped = [win[rows] / (jnp.minimum(pos + w // 2, seq_len) - jnp.maximum(pos - w // 2, 0)).astype(F32)
                   for rows, pos in zip((slice(0, POOL_HALO), slice(TOKEN_TILE - POOL_HALO, TOKEN_TILE)),
                                        edge_pos)]
        mean = jnp.concatenate(
            [clipped[0], win[POOL_HALO:TOKEN_TILE - POOL_HALO] * (1.0 / w), clipped[1]], axis=0)
        pooled = (mean - u[:, cols]).astype(BF16)
        outs.append(_mm(pooled, wpool_ref[cols, :]))
    return jnp.concatenate(outs, axis=-1) * pscale_ref[...]


def _merge_kernel(x_ref, attn_ref, u_ref, up_ref, un_ref, sg_ref,
                  gate_mix_ref, shift_ref, scale_ref, gate_ffn_ref, g_ref, gfin_ref, pscale_ref,
                  wpool_hbm, wba_hbm, wbp_hbm, wout_hbm, wgu_hbm, wd_hbm,
                  o_ref, wpool_ref, wba_ref, wbp_ref, wout_ref, wgu_ref, wd_ref, *, tiles_per_seq, seq_len):
    def load_weights():
        _load_bf16(wpool_hbm.at[0], wpool_ref, 256)
        _load_bf16(wba_hbm.at[0], wba_ref, 256)
        _load_bf16(wbp_hbm.at[0], wbp_ref, 256)
        _load_bf16(wout_hbm.at[0], wout_ref, 256)
        _load_ffn_weights(wgu_hbm, wd_hbm, 1, wgu_ref, wd_ref)

    _on_first_step(load_weights)
    attn = jnp.concatenate([attn_ref[hd] for hd in range(N_HEADS)], axis=-1)
    ya = sg_ref[:, :D_MODEL].astype(F32) * _mm(attn, wba_ref[...])
    pos0 = (pl.program_id(0) % tiles_per_seq) * TOKEN_TILE
    pool = _pool_branch(u_ref[...], up_ref[...], un_ref[...], wpool_ref, pscale_ref, pos0, seq_len)
    y = ya + sg_ref[:, D_MODEL:].astype(F32) * _mm(pool.astype(BF16), wbp_ref[...])
    x = x_ref[...] + gate_mix_ref[...] * _mm(y.astype(BF16), wout_ref[...])
    h = _ada_norm(x, g_ref, scale_ref, shift_ref)
    x = x + (0.5 * gate_ffn_ref[...]) * _swiglu([h], wgu_ref, wd_ref)[0]
    o_ref[...] = _rms_norm(x, gfin_ref[...])


def _merge(x, attn, u, sg, mod3, row_of_token, g3, g_final, pool_scale, w_pool, w_ba, w_bp, w_out,
           w_ffn_gu, w_ffn_down, seq_len):
    n_tok = x.shape[0]
    tiles_per_seq = seq_len // TOKEN_TILE
    halo_blocks = TOKEN_TILE // POOL_HALO
    n_halo = n_tok // POOL_HALO
    prev_spec = pl.BlockSpec((POOL_HALO, POOL_WIDTH),
                             lambda i: (jnp.maximum(i * halo_blocks - 1, 0), 0))
    next_spec = pl.BlockSpec((POOL_HALO, POOL_WIDTH),
                             lambda i: (jnp.minimum((i + 1) * halo_blocks, n_halo - 1), 0))
    kern = functools.partial(_merge_kernel, tiles_per_seq=tiles_per_seq, seq_len=seq_len)
    return pl.pallas_call(
        kern,
        grid=(n_tok // TOKEN_TILE,),
        in_specs=[
            _row_spec(D_MODEL),
            pl.BlockSpec((None, N_HEADS, TOKEN_TILE, V_DIM),
                         lambda i: (i // tiles_per_seq, 0, i % tiles_per_seq, 0)),
            _row_spec(POOL_WIDTH), prev_spec, next_spec,
            _row_spec(2 * D_MODEL),
            _mod_spec(row_of_token, TOKEN_TILE, 5), _mod_spec(row_of_token, TOKEN_TILE, 6),
            _mod_spec(row_of_token, TOKEN_TILE, 7), _mod_spec(row_of_token, TOKEN_TILE, 8), _vec_spec(2),
            pl.BlockSpec((1, D_MODEL), lambda i: (0, 0)),
            pl.BlockSpec((1, POOL_WIDTH), lambda i: (0, 0)),
            _HBM, _HBM, _HBM, _HBM, _HBM, _HBM,
        ],
        out_specs=_row_spec(D_MODEL),
        out_shape=jax.ShapeDtypeStruct((n_tok, D_MODEL), F32),
        scratch_shapes=[pltpu.VMEM((POOL_WIDTH, POOL_GROUP_DIM), BF16),
                        pltpu.VMEM((ATTN_WIDTH, D_MODEL), BF16), pltpu.VMEM((POOL_WIDTH, D_MODEL), BF16),
                        pltpu.VMEM((D_MODEL, D_MODEL), BF16)] + _ffn_weight_scratch(),
        compiler_params=_SEQUENTIAL,
        name="merge",
    )(x, attn, u, u, u, sg, mod3, mod3, mod3, mod3, g3, g_final, pool_scale,
      w_pool, w_ba, w_bp, w_out, w_ffn_gu, w_ffn_down)


def _rope_tables(seq_len):
    half = ROPE_AXIS_DIM // 2
    n_rows = seq_len // GRID_W
    freqs = ROPE_BASE ** (-jnp.arange(half, dtype=F32) / half)
    ang_row = jnp.arange(n_rows, dtype=F32)[:, None] * freqs[None, :]
    ang_col = jnp.arange(GRID_W, dtype=F32)[:, None] * freqs[None, :]
    per_row = lambda tab: jnp.broadcast_to(tab[:, None, :], (n_rows, GRID_W, half)).reshape(seq_len, half)
    per_col = lambda tab: jnp.broadcast_to(tab[None, :, :], (n_rows, GRID_W, half)).reshape(seq_len, half)
    cr, sr = per_row(jnp.cos(ang_row)), per_row(jnp.sin(ang_row))
    cc, sc = per_col(jnp.cos(ang_col)), per_col(jnp.sin(ang_col))
    zero = jnp.zeros_like(sr)
    lanes = lambda a, b, c, d: jnp.concatenate([a, b, c, d] * 2, axis=-1)
    return lanes(cr, cr, cc, cc), lanes(-sr, zero, -sc, zero), lanes(zero, sr, zero, sc)


def kernel(x, c, ctx, c_ctx, w_mod, b_mod, g_norm, w_ffn_gu, w_ffn_down, w_in, lambda_q1, lambda_k1,
           lambda_q2, lambda_k2, g_subln, w_pool, pool_scale, w_branch_attn, w_branch_pool, w_out,
           g_final):
    bsz, seq_len, _ = x.shape
    ctx_len = ctx.shape[1]
    assert w_mod.shape[0] == 1, "single-layer block"
    assert seq_len % max(TOKEN_TILE, FFN_TILE) == 0 and bsz + 1 <= MOD_ROWS

    cc = jnp.concatenate([c, c_ctx[None, :], jnp.zeros((MOD_ROWS - bsz - 1, D_MODEL), F32)], axis=0)
    mod3 = _modulation(cc, w_mod[0], b_mod).reshape(MOD_ROWS, 1, N_MOD * D_MODEL)
    g3 = g_norm[0].reshape(3, 1, D_MODEL)

    lat_row = lambda first_token: first_token // seq_len
    ctx_row = lambda first_token: bsz

    lat = _ffn(x.reshape(bsz * seq_len, D_MODEL), mod3, lat_row, g3, 0, 0, w_ffn_gu, w_ffn_down, 0)
    cx = _ffn(ctx.reshape(bsz * ctx_len, D_MODEL), mod3, ctx_row, g3, 0, 0, w_ffn_gu, w_ffn_down, 0)

    q, k_l, v_l, u, sg = _mix_in_lat(lat, mod3, lat_row, g3, w_in, _rope_tables(seq_len), seq_len)
    k_c, v_c = _mix_in_ctx(cx, mod3, ctx_row, g3, w_in, ctx_len)

    lams = [v.reshape(1, HEAD_DIM) for v in (lambda_q1, lambda_k1, lambda_q2, lambda_k2)]
    attn = _attention(lams, g_subln.reshape(1, V_DIM), q, k_c, k_l, v_c, v_l)

    out = _merge(lat, attn, u, sg, mod3, lat_row, g3,
                 g_final.reshape(1, D_MODEL), pool_scale.reshape(1, POOL_WIDTH),
                 w_pool.reshape(1, POOL_WIDTH, POOL_GROUP_DIM), w_branch_attn, w_branch_pool, w_out,
                 w_ffn_gu, w_ffn_down, seq_len)
    return out.reshape(bsz, seq_len, D_MODEL)
```

```python
import functools
import math

import jax
import jax.numpy as jnp
from jax import lax
from jax.experimental import pallas as pl
from jax.experimental.pallas import tpu as pltpu

F32 = jnp.float32
BF16 = jnp.bfloat16

D_MODEL = 1024
N_HEADS = 8
HEAD_DIM = 64
V_DIM = 2 * HEAD_DIM
QK_WIDTH = N_HEADS * 2 * HEAD_DIM
ATTN_WIDTH = N_HEADS * V_DIM
POOL_WINDOWS = (2, 4, 8, 16)
POOL_GROUP_DIM = 128
POOL_WIDTH = len(POOL_WINDOWS) * POOL_GROUP_DIM
POOL_HALO = 8
D_FF = 2816
GRID_W = 64
ROPE_BASE = 10000.0
ROPE_AXIS_DIM = HEAD_DIM // 2
N_MOD = 9
EPS = 1e-6
LAM_INIT = 0.8 - 0.6 * math.exp(-0.3 * 0)

Q_OFF = 0
K_OFF = Q_OFF + QK_WIDTH
V_OFF = K_OFF + QK_WIDTH
P_OFF = V_OFF + ATTN_WIDTH
G_OFF = P_OFF + POOL_WIDTH
IN_COLS = G_OFF + 2 * D_MODEL

LOG2E = 1.4426950408889634

VMEM_LIMIT_BYTES = 56 * 1024 * 1024
TOKEN_TILE = 512
FFN_TILE = 1024
MIX_TILE = 1024
CHAIN_ROWS = 512
FF_CHUNK = 256
Q_TILE = 512
ATTN_HEADS_PER_STEP = 4
MOD_ROWS = 24


def _sigmoid(x):
    return 1.0 / (1.0 + jnp.exp(-x))


def _rms_norm(x, g):
    return x * lax.rsqrt(jnp.mean(x * x, axis=-1, keepdims=True) + EPS) * g


def _ada_norm(x, g_ref, scale_ref, shift_ref):
    w = g_ref[...] * (1.0 + scale_ref[...])
    inv = lax.rsqrt(jnp.mean(x * x, axis=-1, keepdims=True) + EPS)
    return (x * inv * w + shift_ref[...]).astype(BF16)


def _mm(a, b):
    return jnp.dot(a, b, preferred_element_type=F32)


def _row_chains(tile):
    return [slice(r, r + CHAIN_ROWS) for r in range(0, tile, CHAIN_ROWS)]


def _swiglu_chunks(hs, accs, chunk_ids, wgu_ref, wd_ref):
    accs = list(accs)
    for j in chunk_ids:
        lo = j * FF_CHUNK
        for c, h in enumerate(hs):
            a = _mm(h, wgu_ref[:, lo:lo + FF_CHUNK])
            b = _mm(h, wgu_ref[:, D_FF + lo:D_FF + lo + FF_CHUNK])
            t = (a * _sigmoid(a) * b).astype(BF16)
            part = _mm(t, wd_ref[lo:lo + FF_CHUNK, :])
            accs[c] = part if accs[c] is None else accs[c] + part
    return accs


def _swiglu(hs, wgu_ref, wd_ref):
    return _swiglu_chunks(hs, [None] * len(hs), range(D_FF // FF_CHUNK), wgu_ref, wd_ref)


def _load_bf16(src, dst, chunk_rows):
    n_rows, n_cols = dst.shape
    assert src.shape == dst.shape and n_rows % chunk_rows == 0
    n_chunks = n_rows // chunk_rows

    def body(stage, sem):
        def copy(c):
            return pltpu.make_async_copy(src.at[pl.ds(c * chunk_rows, chunk_rows), :],
                                         stage.at[c % 2], sem.at[c % 2])
        copy(0).start()
        for c in range(n_chunks):
            if c + 1 < n_chunks:
                copy(c + 1).start()
            copy(c).wait()
            dst[pl.ds(c * chunk_rows, chunk_rows), :] = stage[c % 2].astype(BF16)

    pl.run_scoped(body, pltpu.VMEM((2, chunk_rows, n_cols), F32), pltpu.SemaphoreType.DMA((2,)))


def _on_first_step(fn):
    pl.when(pl.program_id(0) == 0)(fn)


_HBM = pl.BlockSpec(memory_space=pl.ANY)
_SEQUENTIAL = pltpu.CompilerParams(dimension_semantics=("arbitrary",), vmem_limit_bytes=VMEM_LIMIT_BYTES)


def _mod_kernel(c_ref, w_ref, b_ref, o_ref):
    c = c_ref[...]
    s = (c * _sigmoid(c)).astype(BF16)
    o_ref[...] = _mm(s, w_ref[...].astype(BF16)) + b_ref[...]


def _modulation(cc, w_mod, b_mod):
    n_out = w_mod.shape[1]
    blk = D_MODEL
    return pl.pallas_call(
        _mod_kernel,
        grid=(n_out // blk,),
        in_specs=[
            pl.BlockSpec((MOD_ROWS, D_MODEL), lambda j: (0, 0)),
            pl.BlockSpec((D_MODEL, blk), lambda j: (0, j)),
            pl.BlockSpec((1, blk), lambda j: (0, j)),
        ],
        out_specs=pl.BlockSpec((MOD_ROWS, blk), lambda j: (0, j)),
        out_shape=jax.ShapeDtypeStruct((MOD_ROWS, n_out), F32),
        compiler_params=pltpu.CompilerParams(
            dimension_semantics=("arbitrary",), vmem_limit_bytes=VMEM_LIMIT_BYTES),
        name="mod",
    )(cc, w_mod, b_mod)


def _mod_spec(row_of_token, tile, chunk):
    return pl.BlockSpec((None, 1, D_MODEL), lambda i: (row_of_token(i * tile), 0, chunk))


def _row_spec(width, tile=TOKEN_TILE):
    return pl.BlockSpec((tile, width), lambda i: (i, 0))


def _vec_spec(idx):
    return pl.BlockSpec((None, 1, D_MODEL), lambda i: (idx, 0, 0))


def _load_ffn_weights(wgu_hbm, wd_hbm, half, wgu_ref, wd_ref):
    _load_bf16(wgu_hbm.at[0, half], wgu_ref, 128)
    _load_bf16(wd_hbm.at[0, half], wd_ref, 256)


def _ffn_kernel(x_ref, shift_ref, scale_ref, gate_ref, g_ref, wgu_hbm, wd_hbm, o_ref, wgu_ref, wd_ref,
                *, half):
    _on_first_step(lambda: _load_ffn_weights(wgu_hbm, wd_hbm, half, wgu_ref, wd_ref))
    chains = _row_chains(x_ref.shape[0])
    hs = [_ada_norm(x_ref[rows, :], g_ref, scale_ref, shift_ref)
          for rows in chains]
    for rows, y in zip(chains, _swiglu(hs, wgu_ref, wd_ref)):
        o_ref[rows, :] = x_ref[rows, :] + (0.5 * gate_ref[...]) * y


def _ffn_weight_scratch():
    return [pltpu.VMEM((D_MODEL, 2 * D_FF), BF16), pltpu.VMEM((D_FF, D_MODEL), BF16)]


def _ffn(x, mod3, row_of_token, g3, g_idx, chunk0, w_ffn_gu, w_ffn_down, half):
    n_tok = x.shape[0]
    tile = FFN_TILE
    assert n_tok % tile == 0
    return pl.pallas_call(
        functools.partial(_ffn_kernel, half=half),
        grid=(n_tok // tile,),
        in_specs=[
            _row_spec(D_MODEL, tile),
            _mod_spec(row_of_token, tile, chunk0),
            _mod_spec(row_of_token, tile, chunk0 + 1),
            _mod_spec(row_of_token, tile, chunk0 + 2),
            _vec_spec(g_idx),
            _HBM, _HBM,
        ],
        out_specs=_row_spec(D_MODEL, tile),
        out_shape=jax.ShapeDtypeStruct((n_tok, D_MODEL), F32),
        scratch_shapes=_ffn_weight_scratch(),
        compiler_params=_SEQUENTIAL,
        name="ffn",
    )(x, mod3, mod3, mod3, g3, w_ffn_gu, w_ffn_down)


def _rope(z, cos, sin_lo, sin_hi):
    return z * cos + pltpu.roll(z, V_DIM - 16, 1) * sin_lo + pltpu.roll(z, 16, 1) * sin_hi


def _build_rope_tables(freq_ref, cos_ref, slo_ref, shi_ref):
    n_rows = cos_ref.shape[0] // GRID_W
    lane = lax.broadcasted_iota(jnp.int32, (GRID_W, V_DIM), 1)
    on_row_axis = (lane & (HEAD_DIM - 1)) < ROPE_AXIS_DIM
    low_half = (lane & (ROPE_AXIS_DIM - 1)) < ROPE_AXIS_DIM // 2
    freq = freq_ref[...]
    col_ang = lax.broadcasted_iota(jnp.int32, (GRID_W, V_DIM), 0).astype(F32) * freq
    row_ang = lax.broadcasted_iota(jnp.int32, (n_rows, V_DIM), 0).astype(F32) * freq
    cos_col, sin_col = jnp.cos(col_ang), jnp.sin(col_ang)
    cos_row, sin_row = jnp.cos(row_ang), jnp.sin(row_ang)
    zero = jnp.zeros((GRID_W, V_DIM), F32)
    for r in range(n_rows):
        tokens = slice(r * GRID_W, (r + 1) * GRID_W)
        sin = jnp.where(on_row_axis, sin_row[r:r + 1, :], sin_col)
        cos_ref[tokens, :] = jnp.where(on_row_axis, cos_row[r:r + 1, :], cos_col)
        slo_ref[tokens, :] = jnp.where(low_half, -sin, zero)
        shi_ref[tokens, :] = jnp.where(low_half, zero, sin)


def _mix_in_lat_kernel(x_ref, shift_ref, scale_ref, g_ref, w_hbm, freq_ref,
                       q_ref, k_ref, v_ref, u_ref, sg_ref, w_ref, cos_ref, slo_ref, shi_ref,
                       *, tiles_per_seq):
    def init():
        _load_bf16(w_hbm.at[0], w_ref, 128)
        _build_rope_tables(freq_ref, cos_ref, slo_ref, shi_ref)

    _on_first_step(init)
    tile = x_ref.shape[0]
    chains = _row_chains(tile)
    hs = [_ada_norm(x_ref[rows, :], g_ref, scale_ref, shift_ref)
          for rows in chains]
    q_scale = HEAD_DIM ** -0.5 * LOG2E
    pos0 = (pl.program_id(0) % tiles_per_seq) * tile

    def rope_store(z, rows, out_ref, scale):
        tokens = pl.ds(pl.multiple_of(pos0 + rows.start, CHAIN_ROWS), CHAIN_ROWS)
        cos, slo, shi = cos_ref[tokens, :], slo_ref[tokens, :], shi_ref[tokens, :]
        for hd in range(N_HEADS):
            sl = slice(hd * V_DIM, (hd + 1) * V_DIM)
            r = _rope(z[:, sl], cos, slo, shi)
            out_ref[hd, rows, :] = (r if scale is None else r * scale).astype(BF16)

    for rows, h in zip(chains, hs):
        rope_store(_mm(h, w_ref[:, Q_OFF:K_OFF]), rows, q_ref, q_scale)
    for rows, h in zip(chains, hs):
        rope_store(_mm(h, w_ref[:, K_OFF:V_OFF]), rows, k_ref, None)
    for rows, h in zip(chains, hs):
        v = _mm(h, w_ref[:, V_OFF:P_OFF]).astype(BF16)
        for hd in range(N_HEADS):
            v_ref[hd, rows, :] = v[:, hd * V_DIM:(hd + 1) * V_DIM]
    for rows, h in zip(chains, hs):
        u_ref[rows, :] = _mm(h, w_ref[:, P_OFF:G_OFF])
    for rows, h in zip(chains, hs):
        sg_ref[rows, :] = _sigmoid(_mm(h, w_ref[:, G_OFF:IN_COLS])).astype(BF16)


def _mix_in_lat(x, mod3, row_of_token, g3, w_in, seq_len):
    n_tok = x.shape[0]
    tile = MIX_TILE
    tiles_per_seq = seq_len // tile
    half = ROPE_AXIS_DIM // 2
    freqs = ROPE_BASE ** (-jnp.arange(half, dtype=F32) / half)
    freq_lanes = jnp.tile(freqs, V_DIM // half).reshape(1, V_DIM)
    table = pltpu.VMEM((seq_len, V_DIM), F32)
    bf =lambda w: jax.ShapeDtypeStruct((n_tok, w), BF16)
    row = lambda w: _row_spec(w, tile)
    heads = pl.BlockSpec((None, N_HEADS, tile, V_DIM),
                         lambda i: (i // tiles_per_seq, 0, i % tiles_per_seq, 0))
    heads_shape = jax.ShapeDtypeStruct((n_tok // seq_len, N_HEADS, seq_len, V_DIM), BF16)
    return pl.pallas_call(
        functools.partial(_mix_in_lat_kernel, tiles_per_seq=tiles_per_seq),
        grid=(n_tok // tile,),
        in_specs=[
            row(D_MODEL),
            _mod_spec(row_of_token, tile, 3),
            _mod_spec(row_of_token, tile, 4),
            _vec_spec(1),
            _HBM,
            pl.BlockSpec((1, V_DIM), lambda i: (0, 0)),
        ],
        out_specs=[heads, heads, heads, row(POOL_WIDTH), row(2 * D_MODEL)],
        out_shape=[heads_shape, heads_shape, heads_shape,
                   jax.ShapeDtypeStruct((n_tok, POOL_WIDTH), F32), bf(2 * D_MODEL)],
        scratch_shapes=[pltpu.VMEM((D_MODEL, IN_COLS), BF16), table, table, table],
        compiler_params=_SEQUENTIAL,
        name="mix_in_lat",
    )(x, mod3, mod3, g3, w_in, freq_lanes)


def _mix_in_ctx_kernel(x_ref, shift_ref, scale_ref, g_ref, w_hbm, k_ref, v_ref, wkv_ref):
    _on_first_step(lambda: _load_bf16(w_hbm.at[0, :, pl.ds(K_OFF, P_OFF - K_OFF)], wkv_ref, 256))
    h = _ada_norm(x_ref[...], g_ref, scale_ref, shift_ref)
    samples, _, ctx_len, _ = k_ref.shape
    for cols, out_ref in ((slice(0, QK_WIDTH), k_ref), (slice(QK_WIDTH, QK_WIDTH + ATTN_WIDTH), v_ref)):
        z = _mm(h, wkv_ref[:, cols]).astype(BF16)
        for s in range(samples):
            for hd in range(N_HEADS):
                out_ref[s, hd, :, :] = z[s * ctx_len:(s + 1) * ctx_len, hd * V_DIM:(hd + 1) * V_DIM]


def _mix_in_ctx(x, mod3, row_of_token, g3, w_in, ctx_len):
    n_tok = x.shape[0]
    assert V_OFF == K_OFF + QK_WIDTH and TOKEN_TILE % ctx_len == 0
    samples = TOKEN_TILE // ctx_len
    heads = pl.BlockSpec((samples, N_HEADS, ctx_len, V_DIM), lambda i: (i, 0, 0, 0))
    heads_shape = jax.ShapeDtypeStruct((n_tok // ctx_len, N_HEADS, ctx_len, V_DIM), BF16)
    return pl.pallas_call(
        _mix_in_ctx_kernel,
        grid=(n_tok // TOKEN_TILE,),
        in_specs=[
            _row_spec(D_MODEL),
            _mod_spec(row_of_token, TOKEN_TILE,3),
            _mod_spec(row_of_token, TOKEN_TILE,4),
            _vec_spec(1),
            _HBM,
        ],
        out_specs=[heads, heads],
        out_shape=[heads_shape, heads_shape],
        scratch_shapes=[pltpu.VMEM((D_MODEL, P_OFF - K_OFF), BF16)],
        compiler_params=_SEQUENTIAL,
        name="mix_in_ctx",
    )(x, mod3, mod3, g3, w_in)


def _attn_kernel(lq1_ref, lk1_ref, lq2_ref, lk2_ref, gs_ref, q_ref, kc_ref, kl_ref, vc_ref, vl_ref,
                 o_ref, k_scr, v_scr, sa_ref, sb_ref, ma_ref, mb_ref, oa_ref, ob_ref):
    lam = (jnp.exp(jnp.sum(lq1_ref[...] * lk1_ref[...], axis=-1, keepdims=True))
           - jnp.exp(jnp.sum(lq2_ref[...] * lk2_ref[...], axis=-1, keepdims=True))
           + LAM_INIT)
    gs = gs_ref[...] * (1.0 - LAM_INIT)
    n_heads, seq_len, _ = q_ref.shape
    ctx_len = kc_ref.shape[1]
    n_keys = k_scr.shape[1]
    for hd in range(n_heads):
        k_scr[hd, :ctx_len, :] = kc_ref[hd]
        k_scr[hd, ctx_len:, :] = kl_ref[hd]
        v_scr[hd, :ctx_len, :V_DIM] = vc_ref[hd]
        v_scr[hd, ctx_len:, :V_DIM] = vl_ref[hd]
        v_scr[hd, :, V_DIM:] = jnp.ones((n_keys, V_DIM), BF16)
    first_comp = lax.broadcasted_iota(jnp.int32, (Q_TILE, V_DIM), 1) < HEAD_DIM
    nt = (((1,), (1,)), ((), ()))
    blocks_per_head = seq_len // Q_TILE
    assert blocks_per_head & (blocks_per_head - 1) == 0
    shift = blocks_per_head.bit_length() - 1

    def head_rows(t):
        if isinstance(t, int):
            return t // blocks_per_head, pl.ds((t % blocks_per_head) * Q_TILE, Q_TILE)
        hd = lax.shift_right_logical(t, shift)
        return hd, pl.ds(pl.multiple_of((t - (hd << shift)) * Q_TILE, Q_TILE), Q_TILE)

    def scores(t, s_ref, m_ref):
        hd, rows = head_rows(t)
        q = q_ref[hd, rows, :]
        zero = jnp.zeros_like(q)
        qq = jnp.concatenate([jnp.where(first_comp, q, zero), jnp.where(first_comp, zero, q)], axis=0)
        s = lax.dot_general(qq, k_scr[hd], nt, preferred_element_type=F32)
        s_ref[...] = s
        m_ref[...] = jnp.max(s, axis=-1, keepdims=True)

    def values(t, s_ref, m_ref, ov_ref):
        hd, _ = head_rows(t)
        for comp in range(2):
            r = slice(comp * Q_TILE, (comp + 1) * Q_TILE)
            p = jnp.exp2(s_ref[r, :] - m_ref[r, :]).astype(BF16)
            ov_ref[r, :] = _mm(p, v_scr[hd])

    def finish(t, ov_ref):
        hd, rows = head_rows(t)
        on = ov_ref[:, :V_DIM] / ov_ref[:, V_DIM:]
        o = on[:Q_TILE] - lam * on[Q_TILE:]
        o_ref[hd, rows, :] = _rms_norm(o, gs).astype(BF16)

    n_blocks = n_heads * blocks_per_head
    assert n_blocks % 2 == 0 and n_blocks >= 4
    scores(0, sa_ref, ma_ref)
    scores(1, sb_ref, mb_ref)
    values(0, sa_ref, ma_ref, oa_ref)

    def pair(j, carry):
        scores(2 * j + 2, sa_ref, ma_ref)
        values(2 * j + 1, sb_ref, mb_ref, ob_ref)
        finish(2 * j, oa_ref)
        scores(2 * j + 3, sb_ref, mb_ref)
        values(2 * j + 2, sa_ref, ma_ref, oa_ref)
        finish(2 * j + 1, ob_ref)
        return carry

    lax.fori_loop(0, n_blocks // 2 - 1, pair, 0)
    values(n_blocks - 1, sb_ref, mb_ref, ob_ref)
    finish(n_blocks - 2, oa_ref)
    finish(n_blocks - 1, ob_ref)


def _attention(lams, gs, q, kc, kl, vc, vl):
    bsz, _, seq_len, _ = q.shape
    ctx_len = kc.shape[2]
    n_keys = ctx_len + seq_len
    hg = ATTN_HEADS_PER_STEP
    lam_spec = pl.BlockSpec((1, HEAD_DIM), lambda b, g: (0, 0))
    lat_spec = pl.BlockSpec((None, hg, seq_len, V_DIM), lambda b, g: (b, g, 0, 0))
    ctx_spec = pl.BlockSpec((None, hg, ctx_len, V_DIM), lambda b, g: (b, g, 0, 0))
    return pl.pallas_call(
        _attn_kernel,
        grid=(bsz, N_HEADS // hg),
        in_specs=[lam_spec, lam_spec, lam_spec, lam_spec,
                  pl.BlockSpec((1, V_DIM), lambda b, g: (0, 0)),
                  lat_spec, ctx_spec, lat_spec, ctx_spec, lat_spec],
        out_specs=lat_spec,
        out_shape=jax.ShapeDtypeStruct((bsz, N_HEADS, seq_len, V_DIM), BF16),
        scratch_shapes=[pltpu.VMEM((hg, n_keys, V_DIM), BF16),
                        pltpu.VMEM((hg, n_keys, 2 * V_DIM), BF16),
                        pltpu.VMEM((2 * Q_TILE, n_keys), F32), pltpu.VMEM((2 * Q_TILE, n_keys), F32),
                        pltpu.VMEM((2 * Q_TILE, 1), F32), pltpu.VMEM((2 * Q_TILE, 1), F32),
                        pltpu.VMEM((2 * Q_TILE, 2 * V_DIM), F32), pltpu.VMEM((2 * Q_TILE, 2 * V_DIM), F32)],
        compiler_params=pltpu.CompilerParams(
            dimension_semantics=("parallel", "parallel"), vmem_limit_bytes=VMEM_LIMIT_BYTES),
        name="attn",
    )(*lams, gs, q, kc, kl, vc, vl)


def _pool_branch(u, u_prev, u_next, wpool_ref, pscale_ref, pos0, seq_len):
    u_prev = jnp.where(pos0 > 0, u_prev, 0.0)
    u_next = jnp.where(pos0 + TOKEN_TILE < seq_len, u_next, 0.0)
    ext = jnp.concatenate([u_prev, u, u_next], axis=0)
    n_ext = ext.shape[0]
    edge = lax.broadcasted_iota(jnp.int32, (POOL_HALO, POOL_GROUP_DIM), 0)
    edge_pos = (pos0 + edge, pos0 + (TOKEN_TILE - POOL_HALO) + edge)
    outs = []
    for g, w in enumerate(POOL_WINDOWS):
        cols = slice(g * POOL_GROUP_DIM, (g + 1) * POOL_GROUP_DIM)
        fwd = ext[:, cols]
        span = 1
        while 2 * span < w:
            fwd = fwd + pltpu.roll(fwd, n_ext - span, 0)
            span *= 2
        win = (fwd + pltpu.roll(fwd, span, 0))[POOL_HALO:POOL_HALO + TOKEN_TILE]
        clipped = [win[rows] / (jnp.minimum(pos + w // 2, seq_len) - jnp.maximum(pos - w // 2, 0)).astype(F32)
                   for rows, pos in zip((slice(0, POOL_HALO), slice(TOKEN_TILE - POOL_HALO, TOKEN_TILE)),
                                        edge_pos)]
        mean = jnp.concatenate(
            [clipped[0], win[POOL_HALO:TOKEN_TILE - POOL_HALO] * (1.0 / w), clipped[1]], axis=0)
        pooled = (mean - u[:, cols]).astype(BF16)
        outs.append(_mm(pooled, wpool_ref[cols, :]))
    return jnp.concatenate(outs, axis=-1) * pscale_ref[...]


def _merge_kernel(x_ref, attn_ref, u_ref, up_ref, un_ref, sg_ref,
                  gate_mix_ref, shift_ref, scale_ref, gate_ffn_ref, g_ref, gfin_ref, pscale_ref,
                  wpool_hbm, wba_hbm, wbp_hbm, wout_hbm, wgu_hbm, wd_hbm,
                  o_ref, wpool_ref, wba_ref, wbp_ref, wout_ref, wgu_ref, wd_ref, *, tiles_per_seq, seq_len):
    def load_weights():
        _load_bf16(wpool_hbm.at[0], wpool_ref, 256)
        _load_bf16(wba_hbm.at[0], wba_ref, 256)
        _load_bf16(wbp_hbm.at[0], wbp_ref, 256)
        _load_bf16(wout_hbm.at[0], wout_ref, 256)
        _load_ffn_weights(wgu_hbm, wd_hbm, 1, wgu_ref, wd_ref)

    _on_first_step(load_weights)
    attn = jnp.concatenate([attn_ref[hd] for hd in range(N_HEADS)], axis=-1)
    ya = sg_ref[:, :D_MODEL].astype(F32) * _mm(attn, wba_ref[...])
    pos0 = (pl.program_id(0) % tiles_per_seq) * TOKEN_TILE
    pool = _pool_branch(u_ref[...], up_ref[...], un_ref[...], wpool_ref, pscale_ref, pos0, seq_len)
    y = ya + sg_ref[:, D_MODEL:].astype(F32) * _mm(pool.astype(BF16), wbp_ref[...])
    x = x_ref[...] + gate_mix_ref[...] * _mm(y.astype(BF16), wout_ref[...])
    h = _ada_norm(x, g_ref, scale_ref, shift_ref)
    x = x + (0.5 * gate_ffn_ref[...]) * _swiglu([h], wgu_ref, wd_ref)[0]
    o_ref[...] = _rms_norm(x, gfin_ref[...])


def _merge(x, attn, u, sg, mod3, row_of_token, g3, g_final, pool_scale, w_pool, w_ba, w_bp, w_out,
           w_ffn_gu, w_ffn_down, seq_len):
    n_tok = x.shape[0]
    tiles_per_seq = seq_len // TOKEN_TILE
    halo_blocks = TOKEN_TILE // POOL_HALO
    n_halo = n_tok // POOL_HALO
    prev_spec = pl.BlockSpec((POOL_HALO, POOL_WIDTH),
                             lambda i: (jnp.maximum(i * halo_blocks - 1, 0), 0))
    next_spec = pl.BlockSpec((POOL_HALO, POOL_WIDTH),
                             lambda i: (jnp.minimum((i + 1) * halo_blocks, n_halo - 1), 0))
    kern = functools.partial(_merge_kernel, tiles_per_seq=tiles_per_seq, seq_len=seq_len)
    return pl.pallas_call(
        kern,
        grid=(n_tok // TOKEN_TILE,),
        in_specs=[
            _row_spec(D_MODEL),
            pl.BlockSpec((None, N_HEADS, TOKEN_TILE, V_DIM),
                         lambda i: (i // tiles_per_seq, 0, i % tiles_per_seq, 0)),
            _row_spec(POOL_WIDTH), prev_spec, next_spec,
            _row_spec(2 * D_MODEL),
            _mod_spec(row_of_token, TOKEN_TILE, 5), _mod_spec(row_of_token, TOKEN_TILE, 6),
            _mod_spec(row_of_token, TOKEN_TILE, 7), _mod_spec(row_of_token, TOKEN_TILE, 8), _vec_spec(2),
            pl.BlockSpec((1, D_MODEL), lambda i: (0, 0)),
            pl.BlockSpec((1, POOL_WIDTH), lambda i: (0, 0)),
            _HBM, _HBM, _HBM, _HBM, _HBM, _HBM,
        ],
        out_specs=_row_spec(D_MODEL),
        out_shape=jax.ShapeDtypeStruct((n_tok, D_MODEL), F32),
        scratch_shapes=[pltpu.VMEM((POOL_WIDTH, POOL_GROUP_DIM), BF16),
                        pltpu.VMEM((ATTN_WIDTH, D_MODEL), BF16), pltpu.VMEM((POOL_WIDTH, D_MODEL), BF16),
                        pltpu.VMEM((D_MODEL, D_MODEL), BF16)] + _ffn_weight_scratch(),
        compiler_params=_SEQUENTIAL,
        name="merge",
    )(x, attn, u, u, u, sg, mod3, mod3, mod3, mod3, g3, g_final, pool_scale,
      w_pool, w_ba, w_bp, w_out, w_ffn_gu, w_ffn_down)


def kernel(x, c, ctx, c_ctx, w_mod, b_mod, g_norm, w_ffn_gu, w_ffn_down, w_in, lambda_q1, lambda_k1,
           lambda_q2, lambda_k2, g_subln, w_pool, pool_scale, w_branch_attn, w_branch_pool, w_out,
           g_final):
    bsz, seq_len, _ = x.shape
    ctx_len = ctx.shape[1]
    assert w_mod.shape[0] == 1, "single-layer block"
    assert seq_len % max(TOKEN_TILE, FFN_TILE) == 0 and bsz + 1 <= MOD_ROWS

    cc = jnp.concatenate([c, c_ctx[None, :], jnp.zeros((MOD_ROWS - bsz - 1, D_MODEL), F32)], axis=0)
    mod3 = _modulation(cc, w_mod[0], b_mod).reshape(MOD_ROWS, 1, N_MOD * D_MODEL)
    g3 = g_norm[0].reshape(3, 1, D_MODEL)

    lat_row = lambda first_token: first_token // seq_len
    ctx_row = lambda first_token: bsz

    lat = _ffn(x.reshape(bsz * seq_len, D_MODEL), mod3, lat_row, g3, 0, 0, w_ffn_gu, w_ffn_down, 0)
    cx = _ffn(ctx.reshape(bsz * ctx_len, D_MODEL), mod3, ctx_row, g3, 0, 0, w_ffn_gu, w_ffn_down, 0)

    q, k_l, v_l, u, sg = _mix_in_lat(lat, mod3, lat_row, g3, w_in, seq_len)
    k_c, v_c = _mix_in_ctx(cx, mod3, ctx_row, g3, w_in, ctx_len)

    lams = [v.reshape(1, HEAD_DIM) for v in (lambda_q1, lambda_k1, lambda_q2, lambda_k2)]
    attn = _attention(lams, g_subln.reshape(1, V_DIM), q, k_c, k_l, v_c, v_l)

    out = _merge(lat, attn, u, sg, mod3, lat_row, g3,
                 g_final.reshape(1, D_MODEL), pool_scale.reshape(1, POOL_WIDTH),
                 w_pool.reshape(1, POOL_WIDTH, POOL_GROUP_DIM), w_branch_attn, w_branch_pool, w_out,
                 w_ffn_gu, w_ffn_down, seq_len)
    return out.reshape(bsz, seq_len, D_MODEL)
```

```python
import functools
import math

import jax
import jax.numpy as jnp
from jax import lax
from jax.experimental import pallas as pl
from jax.experimental.pallas import tpu as pltpu

F32 = jnp.float32
BF16 = jnp.bfloat16

D_MODEL = 1024
N_HEADS = 8
HEAD_DIM = 64
V_DIM = 2 * HEAD_DIM
QK_WIDTH = N_HEADS * 2 * HEAD_DIM
ATTN_WIDTH = N_HEADS * V_DIM
POOL_WINDOWS = (2, 4, 8, 16)
POOL_GROUP_DIM = 128
POOL_WIDTH = len(POOL_WINDOWS) * POOL_GROUP_DIM
POOL_HALO = 8
D_FF = 2816
GRID_W = 64
ROPE_BASE = 10000.0
ROPE_AXIS_DIM = HEAD_DIM // 2
N_MOD = 9
EPS = 1e-6
LAM_INIT = 0.8 - 0.6 * math.exp(-0.3 * 0)

Q_OFF = 0
K_OFF = Q_OFF + QK_WIDTH
V_OFF = K_OFF + QK_WIDTH
P_OFF = V_OFF + ATTN_WIDTH
G_OFF = P_OFF + POOL_WIDTH
IN_COLS = G_OFF + 2 * D_MODEL

LOG2E = 1.4426950408889634

VMEM_LIMIT_BYTES = 60 * 1024 * 1024
TOKEN_TILE = 512
FFN_TILE = 1024
MIX_TILE = 1024
CHAIN_ROWS = 512
FF_CHUNK = 256
Q_TILE = 512
ATTN_HEADS_PER_STEP = 4
MOD_ROWS = 24


def _sigmoid(x):
    return 1.0 / (1.0 + jnp.exp(-x))


def _rms_norm(x, g):
    return x * lax.rsqrt(jnp.mean(x * x, axis=-1, keepdims=True) + EPS) * g


def _ada_norm(x, g_ref, scale_ref, shift_ref):
    w = g_ref[...] * (1.0 + scale_ref[...])
    inv = lax.rsqrt(jnp.mean(x * x, axis=-1, keepdims=True) + EPS)
    return (x * inv * w + shift_ref[...]).astype(BF16)


def _mm(a, b):
    return jnp.dot(a, b, preferred_element_type=F32)


def _row_chains(tile):
    return [slice(r, r + CHAIN_ROWS) for r in range(0, tile, CHAIN_ROWS)]


def _swiglu_chunks(hs, accs, chunk_ids, wgu_ref, wd_ref):
    accs = list(accs)
    for j in chunk_ids:
        lo = j * FF_CHUNK
        for c, h in enumerate(hs):
            a = _mm(h, wgu_ref[:, lo:lo + FF_CHUNK])
            b = _mm(h, wgu_ref[:, D_FF + lo:D_FF + lo + FF_CHUNK])
            t = (a * _sigmoid(a) * b).astype(BF16)
            part = _mm(t, wd_ref[lo:lo + FF_CHUNK, :])
            accs[c] = part if accs[c] is None else accs[c] + part
    return accs


def _swiglu(hs, wgu_ref, wd_ref):
    return _swiglu_chunks(hs, [None] * len(hs), range(D_FF // FF_CHUNK), wgu_ref, wd_ref)


def _load_bf16(src, dst, chunk_rows):
    n_rows, n_cols = dst.shape
    assert src.shape == dst.shape and n_rows % chunk_rows == 0
    n_chunks = n_rows // chunk_rows

    def body(stage, sem):
        def copy(c):
            return pltpu.make_async_copy(src.at[pl.ds(c * chunk_rows, chunk_rows), :],
                                         stage.at[c % 2], sem.at[c % 2])
        copy(0).start()
        for c in range(n_chunks):
            if c + 1 < n_chunks:
                copy(c + 1).start()
            copy(c).wait()
            dst[pl.ds(c * chunk_rows, chunk_rows), :] = stage[c % 2].astype(BF16)

    pl.run_scoped(body, pltpu.VMEM((2, chunk_rows, n_cols), F32), pltpu.SemaphoreType.DMA((2,)))


def _on_first_step(fn):
    pl.when(pl.program_id(0) == 0)(fn)


_HBM = pl.BlockSpec(memory_space=pl.ANY)
_SEQUENTIAL = pltpu.CompilerParams(dimension_semantics=("arbitrary",), vmem_limit_bytes=VMEM_LIMIT_BYTES)


def _mod_kernel(c_ref, w_ref, b_ref, o_ref):
    c = c_ref[...]
    s = (c * _sigmoid(c)).astype(BF16)
    o_ref[...] = _mm(s, w_ref[...].astype(BF16)) + b_ref[...]


def _modulation(cc, w_mod, b_mod):
    n_out = w_mod.shape[1]
    blk = D_MODEL
    return pl.pallas_call(
        _mod_kernel,
        grid=(n_out // blk,),
        in_specs=[
            pl.BlockSpec((MOD_ROWS, D_MODEL), lambda j: (0, 0)),
            pl.BlockSpec((D_MODEL, blk), lambda j: (0, j)),
            pl.BlockSpec((1, blk), lambda j: (0, j)),
        ],
        out_specs=pl.BlockSpec((MOD_ROWS, blk), lambda j: (0, j)),
        out_shape=jax.ShapeDtypeStruct((MOD_ROWS, n_out), F32),
        compiler_params=pltpu.CompilerParams(
            dimension_semantics=("arbitrary",), vmem_limit_bytes=VMEM_LIMIT_BYTES),
        name="mod",
    )(cc, w_mod, b_mod)


def _mod_spec(row_of_token, tile, chunk):
    return pl.BlockSpec((None, 1, D_MODEL), lambda i: (row_of_token(i * tile), 0, chunk))


def _row_spec(width, tile=TOKEN_TILE):
    return pl.BlockSpec((tile, width), lambda i: (i, 0))


def _vec_spec(idx):
    return pl.BlockSpec((None, 1, D_MODEL), lambda i: (idx, 0, 0))


def _load_ffn_weights(wgu_hbm, wd_hbm, half, wgu_ref, wd_ref):
    _load_bf16(wgu_hbm.at[0, half], wgu_ref, 128)
    _load_bf16(wd_hbm.at[0, half], wd_ref, 256)


def _ffn_kernel(x_ref, xc_ref, shift_ref, scale_ref, gate_ref, shift_c_ref, scale_c_ref, gate_c_ref, g_ref,
                wgu_hbm, wd_hbm, o_ref, oc_ref, wgu_ref, wd_ref, *, half):
    _on_first_step(lambda: _load_ffn_weights(wgu_hbm, wd_hbm, half, wgu_ref, wd_ref))
    chains = _row_chains(x_ref.shape[0])
    hs = [_ada_norm(x_ref[rows, :], g_ref, scale_ref, shift_ref) for rows in chains]
    hs[-1] = jnp.concatenate([hs[-1], _ada_norm(xc_ref[...], g_ref, scale_c_ref, shift_c_ref)], axis=0)
    ys = _swiglu(hs, wgu_ref, wd_ref)
    for rows, y in zip(chains, ys):
        o_ref[rows, :] = x_ref[rows, :] + (0.5 * gate_ref[...]) * y[:CHAIN_ROWS]
    oc_ref[...] = xc_ref[...] + (0.5 * gate_c_ref[...]) * ys[-1][CHAIN_ROWS:]


def _ffn_weight_scratch():
    return [pltpu.VMEM((D_MODEL, 2 * D_FF), BF16), pltpu.VMEM((D_FF, D_MODEL), BF16)]


def _ctx_rows_per_step(n_ctx_tok, n_steps):
    assert n_ctx_tok % n_steps == 0 and (n_ctx_tok // n_steps) % 16 == 0
    return n_ctx_tok // n_steps


def _ffn(x, xc, mod3, row_of_token, ctx_row, g3, g_idx, chunk0, w_ffn_gu, w_ffn_down, half):
    n_tok = x.shape[0]
    tile = FFN_TILE
    assert n_tok % tile == 0
    n_steps = n_tok // tile
    c_rows = _ctx_rows_per_step(xc.shape[0], n_steps)
    ctx_mod = lambda chunk: pl.BlockSpec((None, 1, D_MODEL), lambda i: (ctx_row, 0, chunk))
    return pl.pallas_call(
        functools.partial(_ffn_kernel, half=half),
        grid=(n_steps,),
        in_specs=[
            _row_spec(D_MODEL, tile),
            _row_spec(D_MODEL, c_rows),
            _mod_spec(row_of_token, tile, chunk0),
            _mod_spec(row_of_token, tile, chunk0 + 1),
            _mod_spec(row_of_token, tile, chunk0 + 2),
            ctx_mod(chunk0), ctx_mod(chunk0 + 1), ctx_mod(chunk0 + 2),
            _vec_spec(g_idx),
            _HBM, _HBM,
        ],
        out_specs=[_row_spec(D_MODEL, tile), _row_spec(D_MODEL, c_rows)],
        out_shape=[jax.ShapeDtypeStruct((n_tok, D_MODEL), F32),
                   jax.ShapeDtypeStruct(xc.shape, F32)],
        scratch_shapes=_ffn_weight_scratch(),
        compiler_params=_SEQUENTIAL,
        name="ffn",
    )(x, xc, mod3, mod3, mod3, mod3, mod3, mod3, g3, w_ffn_gu, w_ffn_down)


def _rope(z, cos, sin_lo, sin_hi):
    return z * cos + pltpu.roll(z, V_DIM - 16, 1) * sin_lo + pltpu.roll(z, 16, 1) * sin_hi


def _build_rope_tables(freq_ref, cos_ref, slo_ref, shi_ref):
    n_rows = cos_ref.shape[0] // GRID_W
    lane = lax.broadcasted_iota(jnp.int32, (GRID_W, V_DIM), 1)
    on_row_axis = (lane & (HEAD_DIM - 1)) < ROPE_AXIS_DIM
    low_half = (lane & (ROPE_AXIS_DIM - 1)) < ROPE_AXIS_DIM // 2
    freq = freq_ref[...]
    col_ang = lax.broadcasted_iota(jnp.int32, (GRID_W, V_DIM), 0).astype(F32) * freq
    row_ang = lax.broadcasted_iota(jnp.int32, (n_rows, V_DIM), 0).astype(F32) * freq
    cos_col, sin_col = jnp.cos(col_ang), jnp.sin(col_ang)
    cos_row, sin_row = jnp.cos(row_ang), jnp.sin(row_ang)
    zero = jnp.zeros((GRID_W, V_DIM), F32)
    for r in range(n_rows):
        tokens = slice(r * GRID_W, (r + 1) * GRID_W)
        sin = jnp.where(on_row_axis, sin_row[r:r + 1, :], sin_col)
        cos_ref[tokens, :] = jnp.where(on_row_axis, cos_row[r:r + 1, :], cos_col)
        slo_ref[tokens, :] = jnp.where(low_half, -sin, zero)
        shi_ref[tokens, :] = jnp.where(low_half, zero, sin)


def _mix_in_kernel(x_ref, xc_ref, shift_ref, scale_ref, shift_c_ref, scale_c_ref, g_ref, w_hbm, freq_ref,
                   q_ref, k_ref, v_ref, u_ref, sg_ref, kc_ref, vc_ref, w_ref, cos_ref, slo_ref, shi_ref,
                   *, tiles_per_seq):
    def init():
        _load_bf16(w_hbm.at[0], w_ref, 128)
        _build_rope_tables(freq_ref, cos_ref, slo_ref, shi_ref)

    _on_first_step(init)
    tile = x_ref.shape[0]
    chains = _row_chains(tile)
    hs = [_ada_norm(x_ref[rows, :], g_ref, scale_ref, shift_ref) for rows in chains]
    with_ctx = list(hs)
    with_ctx[-1] = jnp.concatenate([hs[-1], _ada_norm(xc_ref[...], g_ref, scale_c_ref, shift_c_ref)], axis=0)
    q_scale = HEAD_DIM ** -0.5 * LOG2E
    pos0 = (pl.program_id(0) % tiles_per_seq) * tile

    def rope_store(z, rows, out_ref, scale):
        tokens = pl.ds(pl.multiple_of(pos0 + rows.start, CHAIN_ROWS), CHAIN_ROWS)
        cos, slo, shi = cos_ref[tokens, :], slo_ref[tokens, :], shi_ref[tokens, :]
        for hd in range(N_HEADS):
            sl = slice(hd * V_DIM, (hd + 1) * V_DIM)
            r = _rope(z[:CHAIN_ROWS, sl], cos, slo, shi)
            out_ref[hd, rows, :] = (r if scale is None else r * scale).astype(BF16)

    def heads_store(z, out_ref, rows):
        for hd in range(N_HEADS):
            out_ref[hd, rows, :] = z[:, hd * V_DIM:(hd + 1) * V_DIM]

    for rows, h in zip(chains, hs):
        rope_store(_mm(h, w_ref[:, Q_OFF:K_OFF]), rows, q_ref, q_scale)
    for rows, h in zip(chains, with_ctx):
        k = _mm(h, w_ref[:, K_OFF:V_OFF])
        rope_store(k, rows, k_ref, None)
        if h.shape[0] > CHAIN_ROWS:
            heads_store(k[CHAIN_ROWS:].astype(BF16), kc_ref, slice(None))
    for rows, h in zip(chains, with_ctx):
        v = _mm(h, w_ref[:, V_OFF:P_OFF]).astype(BF16)
        heads_store(v[:CHAIN_ROWS], v_ref, rows)
        if h.shape[0] > CHAIN_ROWS:
            heads_store(v[CHAIN_ROWS:], vc_ref, slice(None))
    for rows, h in zip(chains, hs):
        u_ref[rows, :] = _mm(h, w_ref[:, P_OFF:G_OFF])
    for rows, h in zip(chains, hs):
        sg_ref[rows, :] = _sigmoid(_mm(h, w_ref[:, G_OFF:IN_COLS])).astype(BF16)


def _mix_in(x, xc, mod3, row_of_token, ctx_row, g3, w_in, seq_len, ctx_len):
    n_tok = x.shape[0]
    tile = MIX_TILE
    n_steps = n_tok // tile
    tiles_per_seq = seq_len // tile
    c_rows = _ctx_rows_per_step(xc.shape[0], n_steps)
    assert ctx_len % c_rows == 0
    c_blocks = ctx_len // c_rows
    half = ROPE_AXIS_DIM // 2
    freqs = ROPE_BASE ** (-jnp.arange(half, dtype=F32) / half)
    freq_lanes = jnp.tile(freqs, V_DIM // half).reshape(1, V_DIM)
    table = pltpu.VMEM((seq_len, V_DIM), F32)
    row = lambda w: _row_spec(w, tile)
    ctx_mod = lambda chunk: pl.BlockSpec((None, 1, D_MODEL), lambda i: (ctx_row, 0, chunk))
    heads = pl.BlockSpec((None, N_HEADS, tile, V_DIM),
                         lambda i: (i // tiles_per_seq, 0, i % tiles_per_seq, 0))
    heads_shape = jax.ShapeDtypeStruct((n_tok // seq_len, N_HEADS, seq_len, V_DIM), BF16)
    ctx_heads = pl.BlockSpec((None, N_HEADS, c_rows, V_DIM), lambda i: (i // c_blocks, 0, i % c_blocks, 0))
    ctx_heads_shape = jax.ShapeDtypeStruct((xc.shape[0] // ctx_len, N_HEADS, ctx_len, V_DIM), BF16)
    return pl.pallas_call(
        functools.partial(_mix_in_kernel, tiles_per_seq=tiles_per_seq),
        grid=(n_steps,),
        in_specs=[
            row(D_MODEL),
            _row_spec(D_MODEL, c_rows),
            _mod_spec(row_of_token, tile, 3),
            _mod_spec(row_of_token, tile, 4),
            ctx_mod(3), ctx_mod(4),
            _vec_spec(1),
            _HBM,
            pl.BlockSpec((1, V_DIM), lambda i: (0, 0)),
        ],
        out_specs=[heads, heads, heads, row(POOL_WIDTH), row(2 * D_MODEL), ctx_heads, ctx_heads],
        out_shape=[heads_shape, heads_shape, heads_shape,
                   jax.ShapeDtypeStruct((n_tok, POOL_WIDTH), F32),
                   jax.ShapeDtypeStruct((n_tok, 2 * D_MODEL), BF16), ctx_heads_shape, ctx_heads_shape],
        scratch_shapes=[pltpu.VMEM((D_MODEL, IN_COLS), BF16), table, table, table],
        compiler_params=_SEQUENTIAL,
        name="mix_in",
    )(x, xc, mod3, mod3, mod3, mod3, g3, w_in, freq_lanes)


def _attn_kernel(lq1_ref, lk1_ref, lq2_ref, lk2_ref, gs_ref, q_ref, kc_ref, kl_ref, vc_ref, vl_ref,
                 o_ref, k_scr, v_scr, sa_ref, sb_ref, ma_ref, mb_ref, oa_ref, ob_ref):
    lam = (jnp.exp(jnp.sum(lq1_ref[...] * lk1_ref[...], axis=-1, keepdims=True))
           - jnp.exp(jnp.sum(lq2_ref[...] * lk2_ref[...], axis=-1, keepdims=True))
           + LAM_INIT)
    gs = gs_ref[...] * (1.0 - LAM_INIT)
    n_heads, seq_len, _ = q_ref.shape
    ctx_len = kc_ref.shape[1]
    n_keys = k_scr.shape[1]
    for hd in range(n_heads):
        k_scr[hd, :ctx_len, :] = kc_ref[hd]
        k_scr[hd, ctx_len:, :] = kl_ref[hd]
        v_scr[hd, :ctx_len, :V_DIM] = vc_ref[hd]
        v_scr[hd, ctx_len:, :V_DIM] = vl_ref[hd]
        v_scr[hd, :, V_DIM:] = jnp.ones((n_keys, V_DIM), BF16)
    first_comp = lax.broadcasted_iota(jnp.int32, (Q_TILE, V_DIM), 1) < HEAD_DIM
    nt = (((1,), (1,)), ((), ()))
    blocks_per_head = seq_len // Q_TILE
    assert blocks_per_head & (blocks_per_head - 1) == 0
    shift = blocks_per_head.bit_length() - 1

    def head_rows(t):
        if isinstance(t, int):
            return t // blocks_per_head, pl.ds((t % blocks_per_head) * Q_TILE, Q_TILE)
        hd = lax.shift_right_logical(t, shift)
        return hd, pl.ds(pl.multiple_of((t - (hd << shift)) * Q_TILE, Q_TILE), Q_TILE)

    def scores(t, s_ref, m_ref):
        hd, rows = head_rows(t)
        q = q_ref[hd, rows, :]
        zero = jnp.zeros_like(q)
        qq = jnp.concatenate([jnp.where(first_comp, q, zero), jnp.where(first_comp, zero, q)], axis=0)
        s = lax.dot_general(qq, k_scr[hd], nt, preferred_element_type=F32)
        s_ref[...] = s
        m_ref[...] = jnp.max(s, axis=-1, keepdims=True)

    def values(t, s_ref, m_ref, ov_ref):
        hd, _ = head_rows(t)
        for comp in range(2):
            r = slice(comp * Q_TILE, (comp + 1) * Q_TILE)
            p = jnp.exp2(s_ref[r, :] - m_ref[r, :]).astype(BF16)
            ov_ref[r, :] = _mm(p, v_scr[hd])

    def finish(t, ov_ref):
        hd, rows = head_rows(t)
        on = ov_ref[:, :V_DIM] / ov_ref[:, V_DIM:]
        o = on[:Q_TILE] - lam * on[Q_TILE:]
        o_ref[hd, rows, :] = _rms_norm(o, gs).astype(BF16)

    n_blocks = n_heads * blocks_per_head
    assert n_blocks % 2 == 0 and n_blocks >= 4
    scores(0, sa_ref, ma_ref)
    scores(1, sb_ref, mb_ref)
    values(0, sa_ref, ma_ref, oa_ref)

    def pair(j, carry):
        scores(2 * j + 2, sa_ref, ma_ref)
        values(2 * j + 1, sb_ref, mb_ref, ob_ref)
        finish(2 * j, oa_ref)
        scores(2 * j + 3, sb_ref, mb_ref)
        values(2 * j + 2, sa_ref, ma_ref, oa_ref)
        finish(2 * j + 1, ob_ref)
        return carry

    lax.fori_loop(0, n_blocks // 2 - 1, pair, 0)
    values(n_blocks - 1, sb_ref, mb_ref, ob_ref)
    finish(n_blocks - 2, oa_ref)
    finish(n_blocks - 1, ob_ref)


def _attention(lams, gs, q, kc, kl, vc, vl):
    bsz, _, seq_len, _ = q.shape
    ctx_len = kc.shape[2]
    n_keys = ctx_len + seq_len
    hg = ATTN_HEADS_PER_STEP
    lam_spec = pl.BlockSpec((1, HEAD_DIM), lambda b, g: (0, 0))
    lat_spec = pl.BlockSpec((None, hg, seq_len, V_DIM), lambda b, g: (b, g, 0, 0))
    ctx_spec = pl.BlockSpec((None, hg, ctx_len, V_DIM), lambda b, g: (b, g, 0, 0))
    return pl.pallas_call(
        _attn_kernel,
        grid=(bsz, N_HEADS // hg),
        in_specs=[lam_spec, lam_spec, lam_spec, lam_spec,
                  pl.BlockSpec((1, V_DIM), lambda b, g: (0, 0)),
                  lat_spec, ctx_spec, lat_spec, ctx_spec, lat_spec],
        out_specs=lat_spec,
        out_shape=jax.ShapeDtypeStruct((bsz, N_HEADS, seq_len, V_DIM), BF16),
        scratch_shapes=[pltpu.VMEM((hg, n_keys, V_DIM), BF16),
                        pltpu.VMEM((hg, n_keys, 2 * V_DIM), BF16),
                        pltpu.VMEM((2 * Q_TILE, n_keys), F32), pltpu.VMEM((2 * Q_TILE, n_keys), F32),
                        pltpu.VMEM((2 * Q_TILE, 1), F32), pltpu.VMEM((2 * Q_TILE, 1), F32),
                        pltpu.VMEM((2 * Q_TILE, 2 * V_DIM), F32), pltpu.VMEM((2 * Q_TILE, 2 * V_DIM), F32)],
        compiler_params=pltpu.CompilerParams(
            dimension_semantics=("parallel", "parallel"), vmem_limit_bytes=VMEM_LIMIT_BYTES),
        name="attn",
    )(*lams, gs, q, kc, kl, vc, vl)


def _pool_branch(u, u_prev, u_next, wpool_ref, pscale_ref, pos0, seq_len):
    u_prev = jnp.where(pos0 > 0, u_prev, 0.0)
    u_next = jnp.where(pos0 + TOKEN_TILE < seq_len, u_next, 0.0)
    ext = jnp.concatenate([u_prev, u, u_next], axis=0)
    n_ext = ext.shape[0]
    edge = lax.broadcasted_iota(jnp.int32, (POOL_HALO, POOL_GROUP_DIM), 0)
    edge_pos = (pos0 + edge, pos0 + (TOKEN_TILE - POOL_HALO) + edge)
    outs = []
    for g, w in enumerate(POOL_WINDOWS):
        cols = slice(g * POOL_GROUP_DIM, (g + 1) * POOL_GROUP_DIM)
        fwd = ext[:, cols]
        span = 1
        while 2 * span < w:
            fwd = fwd + pltpu.roll(fwd, n_ext - span, 0)
            span *= 2
        win = (fwd + pltpu.roll(fwd, span, 0))[POOL_HALO:POOL_HALO + TOKEN_TILE]
        clipped = [win[rows] / (jnp.minimum(pos + w // 2, seq_len) - jnp.maximum(pos - w // 2, 0)).astype(F32)
                   for rows, pos in zip((slice(0, POOL_HALO), slice(TOKEN_TILE - POOL_HALO, TOKEN_TILE)),
                                        edge_pos)]
        mean = jnp.concatenate(
            [clipped[0], win[POOL_HALO:TOKEN_TILE - POOL_HALO] * (1.0 / w), clipped[1]], axis=0)
        pooled = (mean - u[:, cols]).astype(BF16)
        outs.append(_mm(pooled, wpool_ref[cols, :]))
    return jnp.concatenate(outs, axis=-1) * pscale_ref[...]


def _merge_kernel(x_ref, attn_ref, u_ref, up_ref, un_ref, sg_ref,
                  gate_mix_ref, shift_ref, scale_ref, gate_ffn_ref, g_ref, gfin_ref, pscale_ref,
                  wpool_hbm, wba_hbm, wbp_hbm, wout_hbm, wgu_hbm, wd_hbm,
                  o_ref, wpool_ref, wba_ref, wbp_ref, wout_ref, wgu_ref, wd_ref, *, tiles_per_seq, seq_len):
    def load_weights():
        _load_bf16(wpool_hbm.at[0], wpool_ref, 256)
        _load_bf16(wba_hbm.at[0], wba_ref, 256)
        _load_bf16(wbp_hbm.at[0], wbp_ref, 256)
        _load_bf16(wout_hbm.at[0], wout_ref, 256)
        _load_ffn_weights(wgu_hbm, wd_hbm, 1, wgu_ref, wd_ref)

    _on_first_step(load_weights)
    attn = jnp.concatenate([attn_ref[hd] for hd in range(N_HEADS)], axis=-1)
    ya = sg_ref[:, :D_MODEL].astype(F32) * _mm(attn, wba_ref[...])
    pos0 = (pl.program_id(0) % tiles_per_seq) * TOKEN_TILE
    pool = _pool_branch(u_ref[...], up_ref[...], un_ref[...], wpool_ref, pscale_ref, pos0, seq_len)
    y = ya + sg_ref[:, D_MODEL:].astype(F32) * _mm(pool.astype(BF16), wbp_ref[...])
    x = x_ref[...] + gate_mix_ref[...] * _mm(y.astype(BF16), wout_ref[...])
    h = _ada_norm(x, g_ref, scale_ref, shift_ref)
    x = x + (0.5 * gate_ffn_ref[...]) * _swiglu([h], wgu_ref, wd_ref)[0]
    o_ref[...] = _rms_norm(x, gfin_ref[...])


def _merge(x, attn, u, sg, mod3, row_of_token, g3, g_final, pool_scale, w_pool, w_ba, w_bp, w_out,
           w_ffn_gu, w_ffn_down, seq_len):
    n_tok = x.shape[0]
    tiles_per_seq = seq_len // TOKEN_TILE
    halo_blocks = TOKEN_TILE // POOL_HALO
    n_halo = n_tok // POOL_HALO
    prev_spec = pl.BlockSpec((POOL_HALO, POOL_WIDTH),
                             lambda i: (jnp.maximum(i * halo_blocks - 1, 0), 0))
    next_spec = pl.BlockSpec((POOL_HALO, POOL_WIDTH),
                             lambda i: (jnp.minimum((i + 1) * halo_blocks, n_halo - 1), 0))
    kern = functools.partial(_merge_kernel, tiles_per_seq=tiles_per_seq, seq_len=seq_len)
    return pl.pallas_call(
        kern,
        grid=(n_tok // TOKEN_TILE,),
        in_specs=[
            _row_spec(D_MODEL),
            pl.BlockSpec((None, N_HEADS, TOKEN_TILE, V_DIM),
                         lambda i: (i // tiles_per_seq, 0, i % tiles_per_seq, 0)),
            _row_spec(POOL_WIDTH), prev_spec, next_spec,
            _row_spec(2 * D_MODEL),
            _mod_spec(row_of_token, TOKEN_TILE, 5), _mod_spec(row_of_token, TOKEN_TILE, 6),
            _mod_spec(row_of_token, TOKEN_TILE, 7), _mod_spec(row_of_token, TOKEN_TILE, 8), _vec_spec(2),
            pl.BlockSpec((1, D_MODEL), lambda i: (0, 0)),
            pl.BlockSpec((1, POOL_WIDTH), lambda i: (0, 0)),
            _HBM, _HBM, _HBM, _HBM, _HBM, _HBM,
        ],
        out_specs=_row_spec(D_MODEL),
        out_shape=jax.ShapeDtypeStruct((n_tok, D_MODEL), F32),
        scratch_shapes=[pltpu.VMEM((POOL_WIDTH, POOL_GROUP_DIM), BF16),
                        pltpu.VMEM((ATTN_WIDTH, D_MODEL), BF16), pltpu.VMEM((POOL_WIDTH, D_MODEL), BF16),
                        pltpu.VMEM((D_MODEL, D_MODEL), BF16)] + _ffn_weight_scratch(),
        compiler_params=_SEQUENTIAL,
        name="merge",
    )(x, attn, u, u, u, sg, mod3, mod3, mod3, mod3, g3, g_final, pool_scale,
      w_pool, w_ba, w_bp, w_out, w_ffn_gu, w_ffn_down)


def kernel(x, c, ctx, c_ctx, w_mod, b_mod, g_norm, w_ffn_gu, w_ffn_down, w_in, lambda_q1, lambda_k1,
           lambda_q2, lambda_k2, g_subln, w_pool, pool_scale, w_branch_attn, w_branch_pool, w_out,
           g_final):
    bsz, seq_len, _ = x.shape
    ctx_len = ctx.shape[1]
    assert w_mod.shape[0] == 1, "single-layer block"
    assert seq_len % max(TOKEN_TILE, FFN_TILE) == 0 and bsz + 1 <= MOD_ROWS

    cc = jnp.concatenate([c, c_ctx[None, :], jnp.zeros((MOD_ROWS - bsz - 1, D_MODEL), F32)], axis=0)
    mod3 = _modulation(cc, w_mod[0], b_mod).reshape(MOD_ROWS, 1, N_MOD * D_MODEL)
    g3 = g_norm[0].reshape(3, 1, D_MODEL)

    lat_row = lambda first_token: first_token // seq_len
    ctx_row = bsz

    lat, cx = _ffn(x.reshape(bsz * seq_len, D_MODEL), ctx.reshape(bsz * ctx_len, D_MODEL), mod3,
                   lat_row, ctx_row, g3, 0, 0, w_ffn_gu, w_ffn_down, 0)
    q, k_l, v_l, u, sg, k_c, v_c = _mix_in(lat, cx, mod3, lat_row, ctx_row, g3, w_in, seq_len, ctx_len)

    lams = [v.reshape(1, HEAD_DIM) for v in (lambda_q1, lambda_k1, lambda_q2, lambda_k2)]
    attn = _attention(lams, g_subln.reshape(1, V_DIM), q, k_c, k_l, v_c, v_l)

    out = _merge(lat, attn, u, sg, mod3, lat_row, g3,
                 g_final.reshape(1, D_MODEL), pool_scale.reshape(1, POOL_WIDTH),
                 w_pool.reshape(1, POOL_WIDTH, POOL_GROUP_DIM), w_branch_attn, w_branch_pool, w_out,
                 w_ffn_gu, w_ffn_down, seq_len)
    return out.reshape(bsz, seq_len, D_MODEL)
```

```python
import functools
import math

import jax
import jax.numpy as jnp
from jax import lax
from jax.experimental import pallas as pl
from jax.experimental.pallas import tpu as pltpu

F32 = jnp.float32
BF16 = jnp.bfloat16

D_MODEL = 1024
N_HEADS = 8
HEAD_DIM = 64
V_DIM = 2 * HEAD_DIM
QK_WIDTH = N_HEADS * 2 * HEAD_DIM
ATTN_WIDTH = N_HEADS * V_DIM
POOL_WINDOWS = (2, 4, 8, 16)
POOL_GROUP_DIM = 128
POOL_WIDTH = len(POOL_WINDOWS) * POOL_GROUP_DIM
POOL_HALO = 8
D_FF = 2816
GRID_W = 64
ROPE_BASE = 10000.0
ROPE_AXIS_DIM = HEAD_DIM // 2
EPS = 1e-6
LAM_INIT = 0.8 - 0.6 * math.exp(-0.3 * 0)

Q_OFF = 0
K_OFF = Q_OFF + QK_WIDTH
V_OFF = K_OFF + QK_WIDTH
P_OFF = V_OFF + ATTN_WIDTH
G_OFF = P_OFF + POOL_WIDTH
IN_COLS = G_OFF + 2 * D_MODEL

LOG2E = 1.4426950408889634

VMEM_LIMIT_BYTES = 60 * 1024 * 1024
TOKEN_TILE = 512
FFN_TILE = 1024
MIX_TILE = 1024
CHAIN_ROWS = 512
FF_CHUNK = 256
Q_TILE = 512
ATTN_HEADS_PER_STEP = 4
MOD_ROWS = 24
MOD_BLOCK = 2304


def _sigmoid(x):
    return 1.0 / (1.0 + jnp.exp(-x))


def _rms_norm(x, g):
    return x * lax.rsqrt(jnp.mean(x * x, axis=-1, keepdims=True) + EPS) * g


def _ada_norm(x, g_ref, scale_ref, shift_ref):
    w = g_ref[...] * (1.0 + scale_ref[...])
    inv = lax.rsqrt(jnp.mean(x * x, axis=-1, keepdims=True) + EPS)
    return (x * inv * w + shift_ref[...]).astype(BF16)


def _mm(a, b):
    return jnp.dot(a, b, preferred_element_type=F32)


def _row_chains(tile):
    return [slice(r, r + CHAIN_ROWS) for r in range(0, tile, CHAIN_ROWS)]


def _swiglu_chunks(hs, accs, chunk_ids, wgu_ref, wd_ref):
    accs = list(accs)
    for j in chunk_ids:
        lo = j * FF_CHUNK
        for c, h in enumerate(hs):
            a = _mm(h, wgu_ref[:, lo:lo + FF_CHUNK])
            b = _mm(h, wgu_ref[:, D_FF + lo:D_FF + lo + FF_CHUNK])
            t = (a * _sigmoid(a) * b).astype(BF16)
            part = _mm(t, wd_ref[lo:lo + FF_CHUNK, :])
            accs[c] = part if accs[c] is None else accs[c] + part
    return accs


def _swiglu(hs, wgu_ref, wd_ref):
    return _swiglu_chunks(hs, [None] * len(hs), range(D_FF // FF_CHUNK), wgu_ref, wd_ref)


def _load_bf16(src, dst, chunk_rows):
    n_rows, n_cols = dst.shape
    assert src.shape == dst.shape and n_rows % chunk_rows == 0
    n_chunks = n_rows // chunk_rows

    def body(stage, sem):
        def copy(c):
            return pltpu.make_async_copy(src.at[pl.ds(c * chunk_rows, chunk_rows), :],
                                         stage.at[c % 2], sem.at[c % 2])
        copy(0).start()
        for c in range(n_chunks):
            if c + 1 < n_chunks:
                copy(c + 1).start()
            copy(c).wait()
            dst[pl.ds(c * chunk_rows, chunk_rows), :] = stage[c % 2].astype(BF16)

    pl.run_scoped(body, pltpu.VMEM((2, chunk_rows, n_cols), F32), pltpu.SemaphoreType.DMA((2,)))


def _on_first_step(fn):
    pl.when(pl.program_id(0) == 0)(fn)


_HBM = pl.BlockSpec(memory_space=pl.ANY)
_SEQUENTIAL = pltpu.CompilerParams(dimension_semantics=("arbitrary",), vmem_limit_bytes=VMEM_LIMIT_BYTES)


def _mod_kernel(c_ref, w_ref, b_ref, o_ref):
    c = c_ref[...]
    s = (c * _sigmoid(c)).astype(BF16)
    o_ref[:, 0, :] = _mm(s, w_ref[...].astype(BF16)) + b_ref[...]


def _modulation(cc, w_mod, b_mod):
    n_out = w_mod.shape[1]
    blk = MOD_BLOCK
    assert n_out % blk == 0
    return pl.pallas_call(
        _mod_kernel,
        grid=(n_out // blk,),
        in_specs=[
            pl.BlockSpec((MOD_ROWS, D_MODEL), lambda j: (0, 0)),
            pl.BlockSpec((D_MODEL, blk), lambda j: (0, j)),
            pl.BlockSpec((1, blk), lambda j: (0, j)),
        ],
        out_specs=pl.BlockSpec((MOD_ROWS, 1, blk), lambda j: (0, 0, j)),
        out_shape=jax.ShapeDtypeStruct((MOD_ROWS, 1, n_out), F32),
        compiler_params=pltpu.CompilerParams(
            dimension_semantics=("arbitrary",), vmem_limit_bytes=VMEM_LIMIT_BYTES),
        name="mod",
    )(cc, w_mod, b_mod)


def _mod_spec(row_of_token, tile, chunk):
    return pl.BlockSpec((None, 1, D_MODEL), lambda i: (row_of_token(i * tile), 0, chunk))


def _row_spec(width, tile=TOKEN_TILE):
    return pl.BlockSpec((tile, width), lambda i: (i, 0))


def _vec_spec(idx):
    return pl.BlockSpec((None, 1, D_MODEL), lambda i: (idx, 0, 0))


def _load_ffn_weights(wgu_hbm, wd_hbm, half, wgu_ref, wd_ref):
    _load_bf16(wgu_hbm.at[0, half], wgu_ref, 128)
    _load_bf16(wd_hbm.at[0, half], wd_ref, 256)


def _ffn_kernel(x_ref, xc_ref, shift_ref, scale_ref, gate_ref, shift_c_ref, scale_c_ref, gate_c_ref, g_ref,
                wgu_hbm, wd_hbm, o_ref, oc_ref, wgu_ref, wd_ref, *, half):
    _on_first_step(lambda: _load_ffn_weights(wgu_hbm, wd_hbm, half, wgu_ref, wd_ref))
    chains = _row_chains(x_ref.shape[0])
    hs = [_ada_norm(x_ref[rows, :], g_ref, scale_ref, shift_ref) for rows in chains]
    hs[-1] = jnp.concatenate([hs[-1], _ada_norm(xc_ref[...], g_ref, scale_c_ref, shift_c_ref)], axis=0)
    ys = _swiglu(hs, wgu_ref, wd_ref)
    for rows, y in zip(chains, ys):
        o_ref[rows, :] = x_ref[rows, :] + (0.5 * gate_ref[...]) * y[:CHAIN_ROWS]
    oc_ref[...] = xc_ref[...] + (0.5 * gate_c_ref[...]) * ys[-1][CHAIN_ROWS:]


def _ffn_weight_scratch():
    return [pltpu.VMEM((D_MODEL, 2 * D_FF), BF16), pltpu.VMEM((D_FF, D_MODEL), BF16)]


def _ctx_rows_per_step(n_ctx_tok, n_steps):
    assert n_ctx_tok % n_steps == 0 and (n_ctx_tok // n_steps) % 16 == 0
    return n_ctx_tok // n_steps


def _ffn(x, xc, mod3, row_of_token, ctx_row, g3, g_idx, chunk0, w_ffn_gu, w_ffn_down, half):
    n_tok = x.shape[0]
    tile = FFN_TILE
    assert n_tok % tile == 0
    n_steps = n_tok // tile
    c_rows = _ctx_rows_per_step(xc.shape[0], n_steps)
    ctx_mod = lambda chunk: pl.BlockSpec((None, 1, D_MODEL), lambda i: (ctx_row, 0, chunk))
    return pl.pallas_call(
        functools.partial(_ffn_kernel, half=half),
        grid=(n_steps,),
        in_specs=[
            _row_spec(D_MODEL, tile),
            _row_spec(D_MODEL, c_rows),
            _mod_spec(row_of_token, tile, chunk0),
            _mod_spec(row_of_token, tile, chunk0 + 1),
            _mod_spec(row_of_token, tile, chunk0 + 2),
            ctx_mod(chunk0), ctx_mod(chunk0 + 1), ctx_mod(chunk0 + 2),
            _vec_spec(g_idx),
            _HBM, _HBM,
        ],
        out_specs=[_row_spec(D_MODEL, tile), _row_spec(D_MODEL, c_rows)],
        out_shape=[jax.ShapeDtypeStruct((n_tok, D_MODEL), F32),
                   jax.ShapeDtypeStruct(xc.shape, F32)],
        scratch_shapes=_ffn_weight_scratch(),
        compiler_params=_SEQUENTIAL,
        name="ffn",
    )(x, xc, mod3, mod3, mod3, mod3, mod3, mod3, g3, w_ffn_gu, w_ffn_down)


def _rope(z, cos, sin_lo, sin_hi):
    return z * cos + pltpu.roll(z, V_DIM - 16, 1) * sin_lo + pltpu.roll(z, 16, 1) * sin_hi


def _build_rope_tables(freq_ref, cos_ref, slo_ref, shi_ref):
    n_rows = cos_ref.shape[0] // GRID_W
    lane = lax.broadcasted_iota(jnp.int32, (GRID_W, V_DIM), 1)
    on_row_axis = (lane & (HEAD_DIM - 1)) < ROPE_AXIS_DIM
    low_half = (lane & (ROPE_AXIS_DIM - 1)) < ROPE_AXIS_DIM // 2
    freq = freq_ref[...]
    col_ang = lax.broadcasted_iota(jnp.int32, (GRID_W, V_DIM), 0).astype(F32) * freq
    row_ang = lax.broadcasted_iota(jnp.int32, (n_rows, V_DIM), 0).astype(F32) * freq
    cos_col, sin_col = jnp.cos(col_ang), jnp.sin(col_ang)
    cos_row, sin_row = jnp.cos(row_ang), jnp.sin(row_ang)
    zero = jnp.zeros((GRID_W, V_DIM), F32)
    for r in range(n_rows):
        tokens = slice(r * GRID_W, (r + 1) * GRID_W)
        sin = jnp.where(on_row_axis, sin_row[r:r + 1, :], sin_col)
        cos_ref[tokens, :] = jnp.where(on_row_axis, cos_row[r:r + 1, :], cos_col)
        slo_ref[tokens, :] = jnp.where(low_half, -sin, zero)
        shi_ref[tokens, :] = jnp.where(low_half, zero, sin)


def _mix_in_kernel(x_ref, xc_ref, shift_ref, scale_ref, shift_c_ref, scale_c_ref, g_ref, w_hbm, freq_ref,
                   q_ref, k_ref, v_ref, u_ref, sg_ref, kc_ref, vc_ref, w_ref, cos_ref, slo_ref, shi_ref,
                   *, tiles_per_seq):
    def init():
        _load_bf16(w_hbm.at[0], w_ref, 128)
        _build_rope_tables(freq_ref, cos_ref, slo_ref, shi_ref)

    _on_first_step(init)
    tile = x_ref.shape[0]
    chains = _row_chains(tile)
    hs = [_ada_norm(x_ref[rows, :], g_ref, scale_ref, shift_ref) for rows in chains]
    with_ctx = list(hs)
    with_ctx[-1] = jnp.concatenate([hs[-1], _ada_norm(xc_ref[...], g_ref, scale_c_ref, shift_c_ref)], axis=0)
    q_scale = HEAD_DIM ** -0.5 * LOG2E
    pos0 = (pl.program_id(0) % tiles_per_seq) * tile

    def rope_store(z, rows, out_ref, scale):
        tokens = pl.ds(pl.multiple_of(pos0 + rows.start, CHAIN_ROWS), CHAIN_ROWS)
        cos, slo, shi = cos_ref[tokens, :], slo_ref[tokens, :], shi_ref[tokens, :]
        for hd in range(N_HEADS):
            sl = slice(hd * V_DIM, (hd + 1) * V_DIM)
            r = _rope(z[:CHAIN_ROWS, sl], cos, slo, shi)
            out_ref[hd, rows, :] = (r if scale is None else r * scale).astype(BF16)

    def heads_store(z, out_ref, rows):
        for hd in range(N_HEADS):
            out_ref[hd, rows, :] = z[:, hd * V_DIM:(hd + 1) * V_DIM]

    for rows, h in zip(chains, hs):
        rope_store(_mm(h, w_ref[:, Q_OFF:K_OFF]), rows, q_ref, q_scale)
    for rows, h in zip(chains, with_ctx):
        k = _mm(h, w_ref[:, K_OFF:V_OFF])
        rope_store(k, rows, k_ref, None)
        if h.shape[0] > CHAIN_ROWS:
            heads_store(k[CHAIN_ROWS:].astype(BF16), kc_ref, slice(None))
    for rows, h in zip(chains, with_ctx):
        v = _mm(h, w_ref[:, V_OFF:P_OFF]).astype(BF16)
        heads_store(v[:CHAIN_ROWS], v_ref, rows)
        if h.shape[0] > CHAIN_ROWS:
            heads_store(v[CHAIN_ROWS:], vc_ref, slice(None))
    for rows, h in zip(chains, hs):
        u_ref[rows, :] = _mm(h, w_ref[:, P_OFF:G_OFF])
    for rows, h in zip(chains, hs):
        sg_ref[rows, :] = _sigmoid(_mm(h, w_ref[:, G_OFF:IN_COLS])).astype(BF16)


def _mix_in(x, xc, mod3, row_of_token, ctx_row, g3, w_in, seq_len, ctx_len):
    n_tok = x.shape[0]
    tile = MIX_TILE
    n_steps = n_tok // tile
    tiles_per_seq = seq_len // tile
    c_rows = _ctx_rows_per_step(xc.shape[0], n_steps)
    assert ctx_len % c_rows == 0
    c_blocks = ctx_len // c_rows
    half = ROPE_AXIS_DIM // 2
    freqs = ROPE_BASE ** (-jnp.arange(half, dtype=F32) / half)
    freq_lanes = jnp.tile(freqs, V_DIM // half).reshape(1, V_DIM)
    table = pltpu.VMEM((seq_len, V_DIM), F32)
    row = lambda w: _row_spec(w, tile)
    ctx_mod = lambda chunk: pl.BlockSpec((None, 1, D_MODEL), lambda i: (ctx_row, 0, chunk))
    heads = pl.BlockSpec((None, N_HEADS, tile, V_DIM),
                         lambda i: (i // tiles_per_seq, 0, i % tiles_per_seq, 0))
    heads_shape = jax.ShapeDtypeStruct((n_tok // seq_len, N_HEADS, seq_len, V_DIM), BF16)
    ctx_heads = pl.BlockSpec((None, N_HEADS, c_rows, V_DIM), lambda i: (i // c_blocks, 0, i % c_blocks, 0))
    ctx_heads_shape = jax.ShapeDtypeStruct((xc.shape[0] // ctx_len, N_HEADS, ctx_len, V_DIM), BF16)
    return pl.pallas_call(
        functools.partial(_mix_in_kernel, tiles_per_seq=tiles_per_seq),
        grid=(n_steps,),
        in_specs=[
            row(D_MODEL),
            _row_spec(D_MODEL, c_rows),
            _mod_spec(row_of_token, tile, 3),
            _mod_spec(row_of_token, tile, 4),
            ctx_mod(3), ctx_mod(4),
            _vec_spec(1),
            _HBM,
            pl.BlockSpec((1, V_DIM), lambda i: (0, 0)),
        ],
        out_specs=[heads, heads, heads, row(POOL_WIDTH), row(2 * D_MODEL), ctx_heads, ctx_heads],
        out_shape=[heads_shape, heads_shape, heads_shape,
                   jax.ShapeDtypeStruct((n_tok, POOL_WIDTH), F32),
                   jax.ShapeDtypeStruct((n_tok, 2 * D_MODEL), BF16), ctx_heads_shape, ctx_heads_shape],
        scratch_shapes=[pltpu.VMEM((D_MODEL, IN_COLS), BF16), table, table, table],
        compiler_params=_SEQUENTIAL,
        name="mix_in",
    )(x, xc, mod3, mod3, mod3, mod3, g3, w_in, freq_lanes)


def _attn_kernel(lq1_ref, lk1_ref, lq2_ref, lk2_ref, gs_ref, q_ref, kc_ref, kl_ref, vc_ref, vl_ref,
                 o_ref, k_scr, v_scr, sa_ref, sb_ref, ma_ref, mb_ref, oa_ref, ob_ref):
    lam = (jnp.exp(jnp.sum(lq1_ref[...] * lk1_ref[...], axis=-1, keepdims=True))
           - jnp.exp(jnp.sum(lq2_ref[...] * lk2_ref[...], axis=-1, keepdims=True))
           + LAM_INIT)
    gs = gs_ref[...] * (1.0 - LAM_INIT)
    n_heads, seq_len, _ = q_ref.shape
    ctx_len = kc_ref.shape[1]
    n_keys = k_scr.shape[1]
    for hd in range(n_heads):
        k_scr[hd, :ctx_len, :] = kc_ref[hd]
        k_scr[hd, ctx_len:, :] = kl_ref[hd]
        v_scr[hd, :ctx_len, :V_DIM] = vc_ref[hd]
        v_scr[hd, ctx_len:, :V_DIM] = vl_ref[hd]
        v_scr[hd, :, V_DIM:] = jnp.ones((n_keys, V_DIM), BF16)
    first_comp = lax.broadcasted_iota(jnp.int32, (Q_TILE, V_DIM), 1) < HEAD_DIM
    nt = (((1,), (1,)), ((), ()))
    blocks_per_head = seq_len // Q_TILE
    assert blocks_per_head & (blocks_per_head - 1) == 0
    shift = blocks_per_head.bit_length() - 1

    def head_rows(t):
        if isinstance(t, int):
            return t // blocks_per_head, pl.ds((t % blocks_per_head) * Q_TILE, Q_TILE)
        hd = lax.shift_right_logical(t, shift)
        return hd, pl.ds(pl.multiple_of((t - (hd << shift)) * Q_TILE, Q_TILE), Q_TILE)

    def scores(t, s_ref, m_ref):
        hd, rows = head_rows(t)
        q = q_ref[hd, rows, :]
        zero = jnp.zeros_like(q)
        qq = jnp.concatenate([jnp.where(first_comp, q, zero), jnp.where(first_comp, zero, q)], axis=0)
        s = lax.dot_general(qq, k_scr[hd], nt, preferred_element_type=F32)
        s_ref[...] = s
        m_ref[...] = jnp.max(s, axis=-1, keepdims=True)

    def values(t, s_ref, m_ref, ov_ref):
        hd, _ = head_rows(t)
        for comp in range(2):
            r = slice(comp * Q_TILE, (comp + 1) * Q_TILE)
            p = jnp.exp2(s_ref[r, :] - m_ref[r, :]).astype(BF16)
            ov_ref[r, :] = _mm(p, v_scr[hd])

    def finish(t, ov_ref):
        hd, rows = head_rows(t)
        on = ov_ref[:, :V_DIM] / ov_ref[:, V_DIM:]
        o = on[:Q_TILE] - lam * on[Q_TILE:]
        o_ref[hd, rows, :] = _rms_norm(o, gs).astype(BF16)

    n_blocks = n_heads * blocks_per_head
    assert n_blocks % 2 == 0 and n_blocks >= 4
    scores(0, sa_ref, ma_ref)
    scores(1, sb_ref, mb_ref)
    values(0, sa_ref, ma_ref, oa_ref)

    def pair(j, carry):
        scores(2 * j + 2, sa_ref, ma_ref)
        values(2 * j + 1, sb_ref, mb_ref, ob_ref)
        finish(2 * j, oa_ref)
        scores(2 * j + 3, sb_ref, mb_ref)
        values(2 * j + 2, sa_ref, ma_ref, oa_ref)
        finish(2 * j + 1, ob_ref)
        return carry

    lax.fori_loop(0, n_blocks // 2 - 1, pair, 0)
    values(n_blocks - 1, sb_ref, mb_ref, ob_ref)
    finish(n_blocks - 2, oa_ref)
    finish(n_blocks - 1, ob_ref)


def _attention(lams, gs, q, kc, kl, vc, vl):
    bsz, _, seq_len, _ = q.shape
    ctx_len = kc.shape[2]
    n_keys = ctx_len + seq_len
    hg = ATTN_HEADS_PER_STEP
    lam_spec = pl.BlockSpec((1, HEAD_DIM), lambda b, g: (0, 0))
    lat_spec = pl.BlockSpec((None, hg, seq_len, V_DIM), lambda b, g: (b, g, 0, 0))
    ctx_spec = pl.BlockSpec((None, hg, ctx_len, V_DIM), lambda b, g: (b, g, 0, 0))
    return pl.pallas_call(
        _attn_kernel,
        grid=(bsz, N_HEADS // hg),
        in_specs=[lam_spec, lam_spec, lam_spec, lam_spec,
                  pl.BlockSpec((1, V_DIM), lambda b, g: (0, 0)),
                  lat_spec, ctx_spec, lat_spec, ctx_spec, lat_spec],
        out_specs=lat_spec,
        out_shape=jax.ShapeDtypeStruct((bsz, N_HEADS, seq_len, V_DIM), BF16),
        scratch_shapes=[pltpu.VMEM((hg, n_keys, V_DIM), BF16),
                        pltpu.VMEM((hg, n_keys, 2 * V_DIM), BF16),
                        pltpu.VMEM((2 * Q_TILE, n_keys), F32), pltpu.VMEM((2 * Q_TILE, n_keys), F32),
                        pltpu.VMEM((2 * Q_TILE, 1), F32), pltpu.VMEM((2 * Q_TILE, 1), F32),
                        pltpu.VMEM((2 * Q_TILE, 2 * V_DIM), F32), pltpu.VMEM((2 * Q_TILE, 2 * V_DIM), F32)],
        compiler_params=pltpu.CompilerParams(
            dimension_semantics=("parallel", "parallel"), vmem_limit_bytes=VMEM_LIMIT_BYTES),
        name="attn",
    )(*lams, gs, q, kc, kl, vc, vl)


def _pool_branch(u, u_prev, u_next, wpool_ref, pscale_ref, pos0, seq_len):
    u_prev = jnp.where(pos0 > 0, u_prev, 0.0)
    u_next = jnp.where(pos0 + TOKEN_TILE < seq_len, u_next, 0.0)
    ext = jnp.concatenate([u_prev, u, u_next], axis=0)
    n_ext = ext.shape[0]
    edge = lax.broadcasted_iota(jnp.int32, (POOL_HALO, POOL_GROUP_DIM), 0)
    edge_pos = (pos0 + edge, pos0 + (TOKEN_TILE - POOL_HALO) + edge)
    outs = []
    for g, w in enumerate(POOL_WINDOWS):
        cols = slice(g * POOL_GROUP_DIM, (g + 1) * POOL_GROUP_DIM)
        fwd = ext[:, cols]
        span = 1
        while 2 * span < w:
            fwd = fwd + pltpu.roll(fwd, n_ext - span, 0)
            span *= 2
        win = (fwd + pltpu.roll(fwd, span, 0))[POOL_HALO:POOL_HALO + TOKEN_TILE]
        clipped = [win[rows] / (jnp.minimum(pos + w // 2, seq_len) - jnp.maximum(pos - w // 2, 0)).astype(F32)
                   for rows, pos in zip((slice(0, POOL_HALO), slice(TOKEN_TILE - POOL_HALO, TOKEN_TILE)),
                                        edge_pos)]
        mean = jnp.concatenate(
            [clipped[0], win[POOL_HALO:TOKEN_TILE - POOL_HALO] * (1.0 / w), clipped[1]], axis=0)
        pooled = (mean - u[:, cols]).astype(BF16)
        outs.append(_mm(pooled, wpool_ref[cols, :]))
    return jnp.concatenate(outs, axis=-1) * pscale_ref[...]


def _merge_kernel(x_ref, attn_ref, u_ref, up_ref, un_ref, sg_ref,
                  gate_mix_ref, shift_ref, scale_ref, gate_ffn_ref, g_ref, gfin_ref, pscale_ref,
                  wpool_hbm, wba_hbm, wbp_hbm, wout_hbm, wgu_hbm, wd_hbm,
                  o_ref, wpool_ref, wba_ref, wbp_ref, wout_ref, wgu_ref, wd_ref, *, tiles_per_seq, seq_len):
    def load_weights():
        _load_bf16(wpool_hbm.at[0], wpool_ref, 256)
        _load_bf16(wba_hbm.at[0], wba_ref, 256)
        _load_bf16(wbp_hbm.at[0], wbp_ref, 256)
        _load_bf16(wout_hbm.at[0], wout_ref, 256)
        _load_ffn_weights(wgu_hbm, wd_hbm, 1, wgu_ref, wd_ref)

    _on_first_step(load_weights)
    attn = jnp.concatenate([attn_ref[hd] for hd in range(N_HEADS)], axis=-1)
    ya = sg_ref[:, :D_MODEL].astype(F32) * _mm(attn, wba_ref[...])
    pos0 = (pl.program_id(0) % tiles_per_seq) * TOKEN_TILE
    pool = _pool_branch(u_ref[...], up_ref[...], un_ref[...], wpool_ref, pscale_ref, pos0, seq_len)
    y = ya + sg_ref[:, D_MODEL:].astype(F32) * _mm(pool.astype(BF16), wbp_ref[...])
    x = x_ref[...] + gate_mix_ref[...] * _mm(y.astype(BF16), wout_ref[...])
    h = _ada_norm(x, g_ref, scale_ref, shift_ref)
    x = x + (0.5 * gate_ffn_ref[...]) * _swiglu([h], wgu_ref, wd_ref)[0]
    o_ref[...] = _rms_norm(x, gfin_ref[...])


def _merge(x, attn, u, sg, mod3, row_of_token, g3, g_final, pool_scale, w_pool, w_ba, w_bp, w_out,
           w_ffn_gu, w_ffn_down, seq_len):
    n_tok = x.shape[0]
    tiles_per_seq = seq_len // TOKEN_TILE
    halo_blocks = TOKEN_TILE // POOL_HALO
    n_halo = n_tok // POOL_HALO
    prev_spec = pl.BlockSpec((POOL_HALO, POOL_WIDTH),
                             lambda i: (jnp.maximum(i * halo_blocks - 1, 0), 0))
    next_spec = pl.BlockSpec((POOL_HALO, POOL_WIDTH),
                             lambda i: (jnp.minimum((i + 1) * halo_blocks, n_halo - 1), 0))
    kern = functools.partial(_merge_kernel, tiles_per_seq=tiles_per_seq, seq_len=seq_len)
    return pl.pallas_call(
        kern,
        grid=(n_tok // TOKEN_TILE,),
        in_specs=[
            _row_spec(D_MODEL),
            pl.BlockSpec((None, N_HEADS, TOKEN_TILE, V_DIM),
                         lambda i: (i // tiles_per_seq, 0, i % tiles_per_seq, 0)),
            _row_spec(POOL_WIDTH), prev_spec, next_spec,
            _row_spec(2 * D_MODEL),
            _mod_spec(row_of_token, TOKEN_TILE, 5), _mod_spec(row_of_token, TOKEN_TILE, 6),
            _mod_spec(row_of_token, TOKEN_TILE, 7), _mod_spec(row_of_token, TOKEN_TILE, 8), _vec_spec(2),
            pl.BlockSpec((1, D_MODEL), lambda i: (0, 0)),
            pl.BlockSpec((1, POOL_WIDTH), lambda i: (0, 0)),
            _HBM, _HBM, _HBM, _HBM, _HBM, _HBM,
        ],
        out_specs=_row_spec(D_MODEL),
        out_shape=jax.ShapeDtypeStruct((n_tok, D_MODEL), F32),
        scratch_shapes=[pltpu.VMEM((POOL_WIDTH, POOL_GROUP_DIM), BF16),
                        pltpu.VMEM((ATTN_WIDTH, D_MODEL), BF16), pltpu.VMEM((POOL_WIDTH, D_MODEL), BF16),
                        pltpu.VMEM((D_MODEL, D_MODEL), BF16)] + _ffn_weight_scratch(),
        compiler_params=_SEQUENTIAL,
        name="merge",
    )(x, attn, u, u, u, sg, mod3, mod3, mod3, mod3, g3, g_final, pool_scale,
      w_pool, w_ba, w_bp, w_out, w_ffn_gu, w_ffn_down)


def kernel(x, c, ctx, c_ctx, w_mod, b_mod, g_norm, w_ffn_gu, w_ffn_down, w_in, lambda_q1, lambda_k1,
           lambda_q2, lambda_k2, g_subln, w_pool, pool_scale, w_branch_attn, w_branch_pool, w_out,
           g_final):
    bsz, seq_len, _ = x.shape
    ctx_len = ctx.shape[1]
    assert w_mod.shape[0] == 1, "single-layer block"
    assert seq_len % max(TOKEN_TILE, FFN_TILE) == 0 and bsz + 1 <= MOD_ROWS

    cc = jnp.concatenate([c, c_ctx[None, :], jnp.zeros((MOD_ROWS - bsz - 1, D_MODEL), F32)], axis=0)
    mod3 = _modulation(cc, w_mod[0], b_mod)
    g3 = g_norm[0].reshape(3, 1, D_MODEL)

    lat_row = lambda first_token: first_token // seq_len
    ctx_row = bsz

    lat, cx = _ffn(x.reshape(bsz * seq_len, D_MODEL), ctx.reshape(bsz * ctx_len, D_MODEL), mod3,
                   lat_row, ctx_row, g3, 0, 0, w_ffn_gu, w_ffn_down, 0)
    q, k_l, v_l, u, sg, k_c, v_c = _mix_in(lat, cx, mod3, lat_row, ctx_row, g3, w_in, seq_len, ctx_len)

    lams = [v.reshape(1, HEAD_DIM) for v in (lambda_q1, lambda_k1, lambda_q2, lambda_k2)]
    attn = _attention(lams, g_subln.reshape(1, V_DIM), q, k_c, k_l, v_c, v_l)

    out = _merge(lat, attn, u, sg, mod3, lat_row, g3,
                 g_final.reshape(1, D_MODEL), pool_scale.reshape(1, POOL_WIDTH),
                 w_pool.reshape(1, POOL_WIDTH, POOL_GROUP_DIM), w_branch_attn, w_branch_pool, w_out,
                 w_ffn_gu, w_ffn_down, seq_len)
    return out.reshape(bsz, seq_len, D_MODEL)
```

```python
import functools
import math

import jax
import jax.numpy as jnp
from jax import lax
from jax.experimental import pallas as pl
from jax.experimental.pallas import tpu as pltpu

F32 = jnp.float32
BF16 = jnp.bfloat16

D_MODEL = 1024
N_HEADS = 8
HEAD_DIM = 64
V_DIM = 2 * HEAD_DIM
QK_WIDTH = N_HEADS * 2 * HEAD_DIM
ATTN_WIDTH = N_HEADS * V_DIM
POOL_WINDOWS = (2, 4, 8, 16)
POOL_GROUP_DIM = 128
POOL_WIDTH = len(POOL_WINDOWS) * POOL_GROUP_DIM
POOL_HALO = 8
D_FF = 2816
GRID_W = 64
ROPE_BASE = 10000.0
ROPE_AXIS_DIM = HEAD_DIM // 2
EPS = 1e-6
LAM_INIT = 0.8 - 0.6 * math.exp(-0.3 * 0)

Q_OFF = 0
K_OFF = Q_OFF + QK_WIDTH
V_OFF = K_OFF + QK_WIDTH
P_OFF = V_OFF + ATTN_WIDTH
G_OFF = P_OFF + POOL_WIDTH
IN_COLS = G_OFF + 2 * D_MODEL

LOG2E = 1.4426950408889634

VMEM_LIMIT_BYTES = 60 * 1024 * 1024
TOKEN_TILE = 512
FFN_TILE = 1024
MIX_TILE = 1024
CHAIN_ROWS = 512
FF_CHUNK = 256
Q_TILE = 512
ATTN_HEADS_PER_STEP = 4
MOD_ROWS = 24
MOD_BLOCK = 2304


def _sigmoid(x):
    return 0.5 * jnp.tanh(0.5 * x) + 0.5


def _rms_norm(x, g):
    return x * lax.rsqrt(jnp.mean(x * x, axis=-1, keepdims=True) + EPS) * g


def _ada_norm(x, g_ref, scale_ref, shift_ref):
    w = g_ref[...] * (1.0 + scale_ref[...])
    inv = lax.rsqrt(jnp.mean(x * x, axis=-1, keepdims=True) + EPS)
    return (x * inv * w + shift_ref[...]).astype(BF16)


def _mm(a, b):
    return jnp.dot(a, b, preferred_element_type=F32)


def _row_chains(tile):
    return [slice(r, r + CHAIN_ROWS) for r in range(0, tile, CHAIN_ROWS)]


def _swiglu_chunks(hs, accs, chunk_ids, wgu_ref, wd_ref):
    accs = list(accs)
    for j in chunk_ids:
        lo = j * FF_CHUNK
        for c, h in enumerate(hs):
            a = _mm(h, wgu_ref[:, lo:lo + FF_CHUNK])
            b = _mm(h, wgu_ref[:, D_FF + lo:D_FF + lo + FF_CHUNK])
            t = (a * _sigmoid(a) * b).astype(BF16)
            part = _mm(t, wd_ref[lo:lo + FF_CHUNK, :])
            accs[c] = part if accs[c] is None else accs[c] + part
    return accs


def _swiglu(hs, wgu_ref, wd_ref):
    return _swiglu_chunks(hs, [None] * len(hs), range(D_FF // FF_CHUNK), wgu_ref, wd_ref)


def _load_bf16(src, dst, chunk_rows):
    n_rows, n_cols = dst.shape
    assert src.shape == dst.shape and n_rows % chunk_rows == 0
    n_chunks = n_rows // chunk_rows

    def body(stage, sem):
        def copy(c):
            return pltpu.make_async_copy(src.at[pl.ds(c * chunk_rows, chunk_rows), :],
                                         stage.at[c % 2], sem.at[c % 2])
        copy(0).start()
        for c in range(n_chunks):
            if c + 1 < n_chunks:
                copy(c + 1).start()
            copy(c).wait()
            dst[pl.ds(c * chunk_rows, chunk_rows), :] = stage[c % 2].astype(BF16)

    pl.run_scoped(body, pltpu.VMEM((2, chunk_rows, n_cols), F32), pltpu.SemaphoreType.DMA((2,)))


def _on_first_step(fn):
    pl.when(pl.program_id(0) == 0)(fn)


_HBM = pl.BlockSpec(memory_space=pl.ANY)
_SEQUENTIAL = pltpu.CompilerParams(dimension_semantics=("arbitrary",), vmem_limit_bytes=VMEM_LIMIT_BYTES)


def _mod_kernel(c_ref, w_ref, b_ref, o_ref):
    c = c_ref[...]
    s = (c * _sigmoid(c)).astype(BF16)
    o_ref[:, 0, :] = _mm(s, w_ref[...].astype(BF16)) + b_ref[...]


def _modulation(cc, w_mod, b_mod):
    n_out = w_mod.shape[1]
    blk = MOD_BLOCK
    assert n_out % blk == 0
    return pl.pallas_call(
        _mod_kernel,
        grid=(n_out // blk,),
        in_specs=[
            pl.BlockSpec((MOD_ROWS, D_MODEL), lambda j: (0, 0)),
            pl.BlockSpec((D_MODEL, blk), lambda j: (0, j)),
            pl.BlockSpec((1, blk), lambda j: (0, j)),
        ],
        out_specs=pl.BlockSpec((MOD_ROWS, 1, blk), lambda j: (0, 0, j)),
        out_shape=jax.ShapeDtypeStruct((MOD_ROWS, 1, n_out), F32),
        compiler_params=pltpu.CompilerParams(
            dimension_semantics=("arbitrary",), vmem_limit_bytes=VMEM_LIMIT_BYTES),
        name="mod",
    )(cc, w_mod, b_mod)


def _mod_spec(row_of_token, tile, chunk):
    return pl.BlockSpec((None, 1, D_MODEL), lambda i: (row_of_token(i * tile), 0, chunk))


def _row_spec(width, tile=TOKEN_TILE):
    return pl.BlockSpec((tile, width), lambda i: (i, 0))


def _vec_spec(idx):
    return pl.BlockSpec((None, 1, D_MODEL), lambda i: (idx, 0, 0))


def _load_ffn_weights(wgu_hbm, wd_hbm, half, wgu_ref, wd_ref):
    _load_bf16(wgu_hbm.at[0, half], wgu_ref, 128)
    _load_bf16(wd_hbm.at[0, half], wd_ref, 256)


def _ffn_kernel(x_ref, xc_ref, shift_ref, scale_ref, gate_ref, shift_c_ref, scale_c_ref, gate_c_ref, g_ref,
                wgu_hbm, wd_hbm, o_ref, oc_ref, wgu_ref, wd_ref, *, half):
    _on_first_step(lambda: _load_ffn_weights(wgu_hbm, wd_hbm, half, wgu_ref, wd_ref))
    chains = _row_chains(x_ref.shape[0])
    hs = [_ada_norm(x_ref[rows, :], g_ref, scale_ref, shift_ref) for rows in chains]
    hs[-1] = jnp.concatenate([hs[-1], _ada_norm(xc_ref[...], g_ref, scale_c_ref, shift_c_ref)], axis=0)
    ys = _swiglu(hs, wgu_ref, wd_ref)
    for rows, y in zip(chains, ys):
        o_ref[rows, :] = x_ref[rows, :] + (0.5 * gate_ref[...]) * y[:CHAIN_ROWS]
    oc_ref[...] = xc_ref[...] + (0.5 * gate_c_ref[...]) * ys[-1][CHAIN_ROWS:]


def _ffn_weight_scratch():
    return [pltpu.VMEM((D_MODEL, 2 * D_FF), BF16), pltpu.VMEM((D_FF, D_MODEL), BF16)]


def _ctx_rows_per_step(n_ctx_tok, n_steps):
    assert n_ctx_tok % n_steps == 0 and (n_ctx_tok // n_steps) % 16 == 0
    return n_ctx_tok // n_steps


def _ffn(x, xc, mod3, row_of_token, ctx_row, g3, g_idx, chunk0, w_ffn_gu, w_ffn_down, half):
    n_tok = x.shape[0]
    tile = FFN_TILE
    assert n_tok % tile == 0
    n_steps = n_tok // tile
    c_rows = _ctx_rows_per_step(xc.shape[0], n_steps)
    ctx_mod = lambda chunk: pl.BlockSpec((None, 1, D_MODEL), lambda i: (ctx_row, 0, chunk))
    return pl.pallas_call(
        functools.partial(_ffn_kernel, half=half),
        grid=(n_steps,),
        in_specs=[
            _row_spec(D_MODEL, tile),
            _row_spec(D_MODEL, c_rows),
            _mod_spec(row_of_token, tile, chunk0),
            _mod_spec(row_of_token, tile, chunk0 + 1),
            _mod_spec(row_of_token, tile, chunk0 + 2),
            ctx_mod(chunk0), ctx_mod(chunk0 + 1), ctx_mod(chunk0 + 2),
            _vec_spec(g_idx),
            _HBM, _HBM,
        ],
        out_specs=[_row_spec(D_MODEL, tile), _row_spec(D_MODEL, c_rows)],
        out_shape=[jax.ShapeDtypeStruct((n_tok, D_MODEL), F32),
                   jax.ShapeDtypeStruct(xc.shape, F32)],
        scratch_shapes=_ffn_weight_scratch(),
        compiler_params=_SEQUENTIAL,
        name="ffn",
    )(x, xc, mod3, mod3, mod3, mod3, mod3, mod3, g3, w_ffn_gu, w_ffn_down)


def _rope(z, cos, sin_lo, sin_hi):
    return z * cos + pltpu.roll(z, V_DIM - 16, 1) * sin_lo + pltpu.roll(z, 16, 1) * sin_hi


def _build_rope_tables(freq_ref, cos_ref, slo_ref, shi_ref):
    n_rows = cos_ref.shape[0] // GRID_W
    lane = lax.broadcasted_iota(jnp.int32, (GRID_W, V_DIM), 1)
    on_row_axis = (lane & (HEAD_DIM - 1)) < ROPE_AXIS_DIM
    low_half = (lane & (ROPE_AXIS_DIM - 1)) < ROPE_AXIS_DIM // 2
    freq = freq_ref[...]
    col_ang = lax.broadcasted_iota(jnp.int32, (GRID_W, V_DIM), 0).astype(F32) * freq
    row_ang = lax.broadcasted_iota(jnp.int32, (n_rows, V_DIM), 0).astype(F32) * freq
    cos_col, sin_col = jnp.cos(col_ang), jnp.sin(col_ang)
    cos_row, sin_row = jnp.cos(row_ang), jnp.sin(row_ang)
    zero = jnp.zeros((GRID_W, V_DIM), F32)
    for r in range(n_rows):
        tokens = slice(r * GRID_W, (r + 1) * GRID_W)
        sin = jnp.where(on_row_axis, sin_row[r:r + 1, :], sin_col)
        cos_ref[tokens, :] = jnp.where(on_row_axis, cos_row[r:r + 1, :], cos_col)
        slo_ref[tokens, :] = jnp.where(low_half, -sin, zero)
        shi_ref[tokens, :] = jnp.where(low_half, zero, sin)


def _mix_in_kernel(x_ref, xc_ref, shift_ref, scale_ref, shift_c_ref, scale_c_ref, g_ref, w_hbm, freq_ref,
                   q_ref, k_ref, v_ref, u_ref, sg_ref, kc_ref, vc_ref, w_ref, cos_ref, slo_ref, shi_ref,
                   *, tiles_per_seq):
    def init():
        _load_bf16(w_hbm.at[0], w_ref, 128)
        _build_rope_tables(freq_ref, cos_ref, slo_ref, shi_ref)

    _on_first_step(init)
    tile = x_ref.shape[0]
    chains = _row_chains(tile)
    hs = [_ada_norm(x_ref[rows, :], g_ref, scale_ref, shift_ref) for rows in chains]
    with_ctx = list(hs)
    with_ctx[-1] = jnp.concatenate([hs[-1], _ada_norm(xc_ref[...], g_ref, scale_c_ref, shift_c_ref)], axis=0)
    q_scale = HEAD_DIM ** -0.5 * LOG2E
    pos0 = (pl.program_id(0) % tiles_per_seq) * tile

    def rope_store(z, rows, out_ref, scale):
        tokens = pl.ds(pl.multiple_of(pos0 + rows.start, CHAIN_ROWS), CHAIN_ROWS)
        cos, slo, shi = cos_ref[tokens, :], slo_ref[tokens, :], shi_ref[tokens, :]
        for hd in range(N_HEADS):
            sl = slice(hd * V_DIM, (hd + 1) * V_DIM)
            r = _rope(z[:CHAIN_ROWS, sl], cos, slo, shi)
            out_ref[hd, rows, :] = (r if scale is None else r * scale).astype(BF16)

    def heads_store(z, out_ref, rows):
        for hd in range(N_HEADS):
            out_ref[hd, rows, :] = z[:, hd * V_DIM:(hd + 1) * V_DIM]

    for rows, h in zip(chains, hs):
        rope_store(_mm(h, w_ref[:, Q_OFF:K_OFF]), rows, q_ref, q_scale)
    for rows, h in zip(chains, with_ctx):
        k = _mm(h, w_ref[:, K_OFF:V_OFF])
        rope_store(k, rows, k_ref, None)
        if h.shape[0] > CHAIN_ROWS:
            heads_store(k[CHAIN_ROWS:].astype(BF16), kc_ref, slice(None))
    for rows, h in zip(chains, with_ctx):
        v = _mm(h, w_ref[:, V_OFF:P_OFF]).astype(BF16)
        heads_store(v[:CHAIN_ROWS], v_ref, rows)
        if h.shape[0] > CHAIN_ROWS:
            heads_store(v[CHAIN_ROWS:], vc_ref, slice(None))
    for rows, h in zip(chains, hs):
        u_ref[rows, :] = _mm(h, w_ref[:, P_OFF:G_OFF])
    for rows, h in zip(chains, hs):
        sg_ref[rows, :] = _sigmoid(_mm(h, w_ref[:, G_OFF:IN_COLS])).astype(BF16)


def _mix_in(x, xc, mod3, row_of_token, ctx_row, g3, w_in, seq_len, ctx_len):
    n_tok = x.shape[0]
    tile = MIX_TILE
    n_steps = n_tok // tile
    tiles_per_seq = seq_len // tile
    c_rows = _ctx_rows_per_step(xc.shape[0], n_steps)
    assert ctx_len % c_rows == 0
    c_blocks = ctx_len // c_rows
    half = ROPE_AXIS_DIM // 2
    freqs = ROPE_BASE ** (-jnp.arange(half, dtype=F32) / half)
    freq_lanes = jnp.tile(freqs, V_DIM // half).reshape(1, V_DIM)
    table = pltpu.VMEM((seq_len, V_DIM), F32)
    row = lambda w: _row_spec(w, tile)
    ctx_mod = lambda chunk: pl.BlockSpec((None, 1, D_MODEL), lambda i: (ctx_row, 0, chunk))
    heads = pl.BlockSpec((None, N_HEADS, tile, V_DIM),
                         lambda i: (i // tiles_per_seq, 0, i % tiles_per_seq, 0))
    heads_shape = jax.ShapeDtypeStruct((n_tok // seq_len, N_HEADS, seq_len, V_DIM), BF16)
    ctx_heads = pl.BlockSpec((None, N_HEADS, c_rows, V_DIM), lambda i: (i // c_blocks, 0, i % c_blocks, 0))
    ctx_heads_shape = jax.ShapeDtypeStruct((xc.shape[0] // ctx_len, N_HEADS, ctx_len, V_DIM), BF16)
    return pl.pallas_call(
        functools.partial(_mix_in_kernel, tiles_per_seq=tiles_per_seq),
        grid=(n_steps,),
        in_specs=[
            row(D_MODEL),
            _row_spec(D_MODEL, c_rows),
            _mod_spec(row_of_token, tile, 3),
            _mod_spec(row_of_token, tile, 4),
            ctx_mod(3), ctx_mod(4),
            _vec_spec(1),
            _HBM,
            pl.BlockSpec((1, V_DIM), lambda i: (0, 0)),
        ],
        out_specs=[heads, heads, heads, row(POOL_WIDTH), row(2 * D_MODEL), ctx_heads, ctx_heads],
        out_shape=[heads_shape, heads_shape, heads_shape,
                   jax.ShapeDtypeStruct((n_tok, POOL_WIDTH), F32),
                   jax.ShapeDtypeStruct((n_tok, 2 * D_MODEL), BF16), ctx_heads_shape, ctx_heads_shape],
        scratch_shapes=[pltpu.VMEM((D_MODEL, IN_COLS), BF16), table, table, table],
        compiler_params=_SEQUENTIAL,
        name="mix_in",
    )(x, xc, mod3, mod3, mod3, mod3, g3, w_in, freq_lanes)


def _attn_kernel(lq1_ref, lk1_ref, lq2_ref, lk2_ref, gs_ref, q_ref, kc_ref, kl_ref, vc_ref, vl_ref,
                 o_ref, k_scr, v_scr, sa_ref, sb_ref, ma_ref, mb_ref, oa_ref, ob_ref):
    lam = (jnp.exp(jnp.sum(lq1_ref[...] * lk1_ref[...], axis=-1, keepdims=True))
           - jnp.exp(jnp.sum(lq2_ref[...] * lk2_ref[...], axis=-1, keepdims=True))
           + LAM_INIT)
    gs = gs_ref[...] * (1.0 - LAM_INIT)
    n_heads, seq_len, _ = q_ref.shape
    ctx_len = kc_ref.shape[1]
    n_keys = k_scr.shape[1]
    for hd in range(n_heads):
        k_scr[hd, :ctx_len, :] = kc_ref[hd]
        k_scr[hd, ctx_len:, :] = kl_ref[hd]
        v_scr[hd, :ctx_len, :V_DIM] = vc_ref[hd]
        v_scr[hd, ctx_len:, :V_DIM] = vl_ref[hd]
        v_scr[hd, :, V_DIM:] = jnp.ones((n_keys, V_DIM), BF16)
    first_comp = lax.broadcasted_iota(jnp.int32, (Q_TILE, V_DIM), 1) < HEAD_DIM
    nt = (((1,), (1,)), ((), ()))
    blocks_per_head = seq_len // Q_TILE
    assert blocks_per_head & (blocks_per_head - 1) == 0
    shift = blocks_per_head.bit_length() - 1

    def head_rows(t):
        if isinstance(t, int):
            return t // blocks_per_head, pl.ds((t % blocks_per_head) * Q_TILE, Q_TILE)
        hd = lax.shift_right_logical(t, shift)
        return hd, pl.ds(pl.multiple_of((t - (hd << shift)) * Q_TILE, Q_TILE), Q_TILE)

    def scores(t, s_ref, m_ref):
        hd, rows = head_rows(t)
        q = q_ref[hd, rows, :]
        zero = jnp.zeros_like(q)
        qq = jnp.concatenate([jnp.where(first_comp, q, zero), jnp.where(first_comp, zero, q)], axis=0)
        s = lax.dot_general(qq, k_scr[hd], nt, preferred_element_type=F32)
        s_ref[...] = s
        m_ref[...] = jnp.max(s, axis=-1, keepdims=True)

    def values(t, s_ref, m_ref, ov_ref):
        hd, _ = head_rows(t)
        for comp in range(2):
            r = slice(comp * Q_TILE, (comp + 1) * Q_TILE)
            p = jnp.exp2(s_ref[r, :] - m_ref[r, :]).astype(BF16)
            ov_ref[r, :] = _mm(p, v_scr[hd])

    def finish(t, ov_ref):
        hd, rows = head_rows(t)
        on = ov_ref[:, :V_DIM] / ov_ref[:, V_DIM:]
        o = on[:Q_TILE] - lam * on[Q_TILE:]
        o_ref[hd, rows, :] = _rms_norm(o, gs).astype(BF16)

    n_blocks = n_heads * blocks_per_head
    assert n_blocks % 2 == 0 and n_blocks >= 4
    scores(0, sa_ref, ma_ref)
    scores(1, sb_ref, mb_ref)
    values(0, sa_ref, ma_ref, oa_ref)

    def pair(j, carry):
        scores(2 * j + 2, sa_ref, ma_ref)
        values(2 * j + 1, sb_ref, mb_ref, ob_ref)
        finish(2 * j, oa_ref)
        scores(2 * j + 3, sb_ref, mb_ref)
        values(2 * j + 2, sa_ref, ma_ref, oa_ref)
        finish(2 * j + 1, ob_ref)
        return carry

    lax.fori_loop(0, n_blocks // 2 - 1, pair, 0)
    values(n_blocks - 1, sb_ref, mb_ref, ob_ref)
    finish(n_blocks - 2, oa_ref)
    finish(n_blocks - 1, ob_ref)


def _attention(lams, gs, q, kc, kl, vc, vl):
    bsz, _, seq_len, _ = q.shape
    ctx_len = kc.shape[2]
    n_keys = ctx_len + seq_len
    hg = ATTN_HEADS_PER_STEP
    lam_spec = pl.BlockSpec((1, HEAD_DIM), lambda b, g: (0, 0))
    lat_spec = pl.BlockSpec((None, hg, seq_len, V_DIM), lambda b, g: (b, g, 0, 0))
    ctx_spec = pl.BlockSpec((None, hg, ctx_len, V_DIM), lambda b, g: (b, g, 0, 0))
    return pl.pallas_call(
        _attn_kernel,
        grid=(bsz, N_HEADS // hg),
        in_specs=[lam_spec, lam_spec, lam_spec, lam_spec,
                  pl.BlockSpec((1, V_DIM), lambda b, g: (0, 0)),
                  lat_spec, ctx_spec, lat_spec, ctx_spec, lat_spec],
        out_specs=lat_spec,
        out_shape=jax.ShapeDtypeStruct((bsz, N_HEADS, seq_len, V_DIM), BF16),
        scratch_shapes=[pltpu.VMEM((hg, n_keys, V_DIM), BF16),
                        pltpu.VMEM((hg, n_keys, 2 * V_DIM), BF16),
                        pltpu.VMEM((2 * Q_TILE, n_keys), F32), pltpu.VMEM((2 * Q_TILE, n_keys), F32),
                        pltpu.VMEM((2 * Q_TILE, 1), F32), pltpu.VMEM((2 * Q_TILE, 1), F32),
                        pltpu.VMEM((2 * Q_TILE, 2 * V_DIM), F32), pltpu.VMEM((2 * Q_TILE, 2 * V_DIM), F32)],
        compiler_params=pltpu.CompilerParams(
            dimension_semantics=("parallel", "parallel"), vmem_limit_bytes=VMEM_LIMIT_BYTES),
        name="attn",
    )(*lams, gs, q, kc, kl, vc, vl)


def _pool_branch(u, u_prev, u_next, wpool_ref, pscale_ref, pos0, seq_len):
    u_prev = jnp.where(pos0 > 0, u_prev, 0.0)
    u_next = jnp.where(pos0 + TOKEN_TILE < seq_len, u_next, 0.0)
    ext = jnp.concatenate([u_prev, u, u_next], axis=0)
    n_ext = ext.shape[0]
    edge = lax.broadcasted_iota(jnp.int32, (POOL_HALO, POOL_GROUP_DIM), 0)
    edge_pos = (pos0 + edge, pos0 + (TOKEN_TILE - POOL_HALO) + edge)
    outs = []
    for g, w in enumerate(POOL_WINDOWS):
        cols = slice(g * POOL_GROUP_DIM, (g + 1) * POOL_GROUP_DIM)
        fwd = ext[:, cols]
        span = 1
        while 2 * span < w:
            fwd = fwd + pltpu.roll(fwd, n_ext - span, 0)
            span *= 2
        win = (fwd + pltpu.roll(fwd, span, 0))[POOL_HALO:POOL_HALO + TOKEN_TILE]
        clipped = [win[rows] / (jnp.minimum(pos + w // 2, seq_len) - jnp.maximum(pos - w // 2, 0)).astype(F32)
                   for rows, pos in zip((slice(0, POOL_HALO), slice(TOKEN_TILE - POOL_HALO, TOKEN_TILE)),
                                        edge_pos)]
        mean = jnp.concatenate(
            [clipped[0], win[POOL_HALO:TOKEN_TILE - POOL_HALO] * (1.0 / w), clipped[1]], axis=0)
        pooled = (mean - u[:, cols]).astype(BF16)
        outs.append(_mm(pooled, wpool_ref[cols, :]))
    return jnp.concatenate(outs, axis=-1) * pscale_ref[...]


def _merge_kernel(x_ref, attn_ref, u_ref, up_ref, un_ref, sg_ref,
                  gate_mix_ref, shift_ref, scale_ref, gate_ffn_ref, g_ref, gfin_ref, pscale_ref,
                  wpool_hbm, wba_hbm, wbp_hbm, wout_hbm, wgu_hbm, wd_hbm,
                  o_ref, wpool_ref, wba_ref, wbp_ref, wout_ref, wgu_ref, wd_ref, *, tiles_per_seq, seq_len):
    def load_weights():
        _load_bf16(wpool_hbm.at[0], wpool_ref, 256)
        _load_bf16(wba_hbm.at[0], wba_ref, 256)
        _load_bf16(wbp_hbm.at[0], wbp_ref, 256)
        _load_bf16(wout_hbm.at[0], wout_ref, 256)
        _load_ffn_weights(wgu_hbm, wd_hbm, 1, wgu_ref, wd_ref)

    _on_first_step(load_weights)
    halves = (slice(0, TOKEN_TILE // 2), slice(TOKEN_TILE // 2, TOKEN_TILE))
    pos0 = (pl.program_id(0) % tiles_per_seq) * TOKEN_TILE
    pool = _pool_branch(u_ref[...], up_ref[...], un_ref[...], wpool_ref, pscale_ref, pos0, seq_len)
    ys = [sg_ref[rows, :D_MODEL].astype(F32)
          * _mm(jnp.concatenate([attn_ref[hd, rows, :] for hd in range(N_HEADS)], axis=-1), wba_ref[...])
          for rows in halves]
    ys = [y + sg_ref[rows, D_MODEL:].astype(F32) * _mm(pool[rows].astype(BF16), wbp_ref[...])
          for rows, y in zip(halves, ys)]
    xs = [x_ref[rows, :] + gate_mix_ref[...] * _mm(y.astype(BF16), wout_ref[...])
          for rows, y in zip(halves, ys)]
    x = jnp.concatenate(xs, axis=0)
    h = jnp.concatenate([_ada_norm(xh, g_ref, scale_ref, shift_ref) for xh in xs], axis=0)
    x = x + (0.5 * gate_ffn_ref[...]) * _swiglu([h], wgu_ref, wd_ref)[0]
    o_ref[...] = _rms_norm(x, gfin_ref[...])


def _merge(x, attn, u, sg, mod3, row_of_token, g3, g_final, pool_scale, w_pool, w_ba, w_bp, w_out,
           w_ffn_gu, w_ffn_down, seq_len):
    n_tok = x.shape[0]
    tiles_per_seq = seq_len // TOKEN_TILE
    halo_blocks = TOKEN_TILE // POOL_HALO
    n_halo = n_tok // POOL_HALO
    prev_spec = pl.BlockSpec((POOL_HALO, POOL_WIDTH),
                             lambda i: (jnp.maximum(i * halo_blocks - 1, 0), 0))
    next_spec = pl.BlockSpec((POOL_HALO, POOL_WIDTH),
                             lambda i: (jnp.minimum((i + 1) * halo_blocks, n_halo - 1), 0))
    kern = functools.partial(_merge_kernel, tiles_per_seq=tiles_per_seq, seq_len=seq_len)
    return pl.pallas_call(
        kern,
        grid=(n_tok // TOKEN_TILE,),
        in_specs=[
            _row_spec(D_MODEL),
            pl.BlockSpec((None, N_HEADS, TOKEN_TILE, V_DIM),
                         lambda i: (i // tiles_per_seq, 0, i % tiles_per_seq, 0)),
            _row_spec(POOL_WIDTH), prev_spec, next_spec,
            _row_spec(2 * D_MODEL),
            _mod_spec(row_of_token, TOKEN_TILE, 5), _mod_spec(row_of_token, TOKEN_TILE, 6),
            _mod_spec(row_of_token, TOKEN_TILE, 7), _mod_spec(row_of_token, TOKEN_TILE, 8), _vec_spec(2),
            pl.BlockSpec((1, D_MODEL), lambda i: (0, 0)),
            pl.BlockSpec((1, POOL_WIDTH), lambda i: (0, 0)),
            _HBM, _HBM, _HBM, _HBM, _HBM, _HBM,
        ],
        out_specs=_row_spec(D_MODEL),
        out_shape=jax.ShapeDtypeStruct((n_tok, D_MODEL), F32),
        scratch_shapes=[pltpu.VMEM((POOL_WIDTH, POOL_GROUP_DIM), BF16),
                        pltpu.VMEM((ATTN_WIDTH, D_MODEL), BF16), pltpu.VMEM((POOL_WIDTH, D_MODEL), BF16),
                        pltpu.VMEM((D_MODEL, D_MODEL), BF16)] + _ffn_weight_scratch(),
        compiler_params=_SEQUENTIAL,
        name="merge",
    )(x, attn, u, u, u, sg, mod3, mod3, mod3, mod3, g3, g_final, pool_scale,
      w_pool, w_ba, w_bp, w_out, w_ffn_gu, w_ffn_down)


def kernel(x, c, ctx, c_ctx, w_mod, b_mod, g_norm, w_ffn_gu, w_ffn_down, w_in, lambda_q1, lambda_k1,
           lambda_q2, lambda_k2, g_subln, w_pool, pool_scale, w_branch_attn, w_branch_pool, w_out,
           g_final):
    bsz, seq_len, _ = x.shape
    ctx_len = ctx.shape[1]
    assert w_mod.shape[0] == 1, "single-layer block"
    assert seq_len % max(TOKEN_TILE, FFN_TILE) == 0 and bsz + 1 <= MOD_ROWS

    cc = jnp.concatenate([c, c_ctx[None, :], jnp.zeros((MOD_ROWS - bsz - 1, D_MODEL), F32)], axis=0)
    mod3 = _modulation(cc, w_mod[0], b_mod)
    g3 = g_norm[0].reshape(3, 1, D_MODEL)

    lat_row = lambda first_token: first_token // seq_len
    ctx_row = bsz

    lat, cx = _ffn(x.reshape(bsz * seq_len, D_MODEL), ctx.reshape(bsz * ctx_len, D_MODEL), mod3,
                   lat_row, ctx_row, g3, 0, 0, w_ffn_gu, w_ffn_down, 0)
    q, k_l, v_l, u, sg, k_c, v_c = _mix_in(lat, cx, mod3, lat_row, ctx_row, g3, w_in, seq_len, ctx_len)

    lams = [v.reshape(1, HEAD_DIM) for v in (lambda_q1, lambda_k1, lambda_q2, lambda_k2)]
    attn = _attention(lams, g_subln.reshape(1, V_DIM), q, k_c, k_l, v_c, v_l)

    out = _merge(lat, attn, u, sg, mod3, lat_row, g3,
                 g_final.reshape(1, D_MODEL), pool_scale.reshape(1, POOL_WIDTH),
                 w_pool.reshape(1, POOL_WIDTH, POOL_GROUP_DIM), w_branch_attn, w_branch_pool, w_out,
                 w_ffn_gu, w_ffn_down, seq_len)
    return out.reshape(bsz, seq_len, D_MODEL)
```

```python
import functools
import math

import jax
import jax.numpy as jnp
from jax import lax
from jax.experimental import pallas as pl
from jax.experimental.pallas import tpu as pltpu

F32 = jnp.float32
BF16 = jnp.bfloat16

D_MODEL = 1024
N_HEADS = 8
HEAD_DIM = 64
V_DIM = 2 * HEAD_DIM
QK_WIDTH = N_HEADS * 2 * HEAD_DIM
ATTN_WIDTH = N_HEADS * V_DIM
POOL_WINDOWS = (2, 4, 8, 16)
POOL_GROUP_DIM = 128
POOL_WIDTH = len(POOL_WINDOWS) * POOL_GROUP_DIM
POOL_HALO = 8
D_FF = 2816
GRID_W = 64
ROPE_BASE = 10000.0
ROPE_AXIS_DIM = HEAD_DIM // 2
EPS = 1e-6
LAM_INIT = 0.8 - 0.6 * math.exp(-0.3 * 0)

Q_OFF = 0
K_OFF = Q_OFF + QK_WIDTH
V_OFF = K_OFF + QK_WIDTH
P_OFF = V_OFF + ATTN_WIDTH
G_OFF = P_OFF + POOL_WIDTH
IN_COLS = G_OFF + 2 * D_MODEL

LOG2E = 1.4426950408889634

VMEM_LIMIT_BYTES = 60 * 1024 * 1024
TOKEN_TILE = 512
FFN_TILE = 1024
MIX_TILE = 1024
CHAIN_ROWS = 512
FF_CHUNK = 256
Q_TILE = 512
ATTN_HEADS_PER_STEP = 4
MOD_ROWS = 24
MOD_BLOCK = 2304


def _sigmoid(x):
    return 0.5 * jnp.tanh(0.5 * x) + 0.5


def _rms_norm(x, g):
    return x * lax.rsqrt(jnp.mean(x * x, axis=-1, keepdims=True) + EPS) * g


def _ada_norm(x, g_ref, scale_ref, shift_ref):
    w = g_ref[...] * (1.0 + scale_ref[...])
    inv = lax.rsqrt(jnp.mean(x * x, axis=-1, keepdims=True) + EPS)
    return (x * inv * w + shift_ref[...]).astype(BF16)


def _mm(a, b):
    return jnp.dot(a, b, preferred_element_type=F32)


def _row_chains(tile):
    return [slice(r, r + CHAIN_ROWS) for r in range(0, tile, CHAIN_ROWS)]


def _swiglu_chunks(hs, accs, chunk_ids, wgu_ref, wd_ref):
    accs = list(accs)
    for j in chunk_ids:
        lo = j * FF_CHUNK
        for c, h in enumerate(hs):
            a = _mm(h, wgu_ref[:, lo:lo + FF_CHUNK])
            b = _mm(h, wgu_ref[:, D_FF + lo:D_FF + lo + FF_CHUNK])
            t = (a * _sigmoid(a) * b).astype(BF16)
            part = _mm(t, wd_ref[lo:lo + FF_CHUNK, :])
            accs[c] = part if accs[c] is None else accs[c] + part
    return accs


def _swiglu(hs, wgu_ref, wd_ref):
    return _swiglu_chunks(hs, [None] * len(hs), range(D_FF // FF_CHUNK), wgu_ref, wd_ref)


def _load_bf16(src, dst, chunk_rows):
    n_rows, n_cols = dst.shape
    assert src.shape == dst.shape and n_rows % chunk_rows == 0
    n_chunks = n_rows // chunk_rows

    def body(stage, sem):
        def copy(c):
            return pltpu.make_async_copy(src.at[pl.ds(c * chunk_rows, chunk_rows), :],
                                         stage.at[c % 2], sem.at[c % 2])
        copy(0).start()
        for c in range(n_chunks):
            if c + 1 < n_chunks:
                copy(c + 1).start()
            copy(c).wait()
            dst[pl.ds(c * chunk_rows, chunk_rows), :] = stage[c % 2].astype(BF16)

    pl.run_scoped(body, pltpu.VMEM((2, chunk_rows, n_cols), F32), pltpu.SemaphoreType.DMA((2,)))


def _on_first_step(fn):
    pl.when(pl.program_id(0) == 0)(fn)


_HBM = pl.BlockSpec(memory_space=pl.ANY)
_SEQUENTIAL = pltpu.CompilerParams(dimension_semantics=("arbitrary",), vmem_limit_bytes=VMEM_LIMIT_BYTES)


def _mod_kernel(c_ref, w_ref, b_ref, o_ref):
    c = c_ref[...]
    s = (c * _sigmoid(c)).astype(BF16)
    o_ref[:, 0, :] = _mm(s, w_ref[...].astype(BF16)) + b_ref[...]


def _modulation(cc, w_mod, b_mod):
    n_out = w_mod.shape[1]
    blk = MOD_BLOCK
    assert n_out % blk == 0
    return pl.pallas_call(
        _mod_kernel,
        grid=(n_out // blk,),
        in_specs=[
            pl.BlockSpec((MOD_ROWS, D_MODEL), lambda j: (0, 0)),
            pl.BlockSpec((D_MODEL, blk), lambda j: (0, j)),
            pl.BlockSpec((1, blk), lambda j: (0, j)),
        ],
        out_specs=pl.BlockSpec((MOD_ROWS, 1, blk), lambda j: (0, 0, j)),
        out_shape=jax.ShapeDtypeStruct((MOD_ROWS, 1, n_out), F32),
        compiler_params=pltpu.CompilerParams(
            dimension_semantics=("arbitrary",), vmem_limit_bytes=VMEM_LIMIT_BYTES),
        name="mod",
    )(cc, w_mod, b_mod)


def _mod_spec(row_of_token, tile, chunk):
    return pl.BlockSpec((None, 1, D_MODEL), lambda i: (row_of_token(i * tile), 0, chunk))


def _row_spec(width, tile=TOKEN_TILE):
    return pl.BlockSpec((tile, width), lambda i: (i, 0))


def _vec_spec(idx):
    return pl.BlockSpec((None, 1, D_MODEL), lambda i: (idx, 0, 0))


def _load_ffn_weights(wgu_hbm, wd_hbm, half, wgu_ref, wd_ref):
    _load_bf16(wgu_hbm.at[0, half], wgu_ref, 128)
    _load_bf16(wd_hbm.at[0, half], wd_ref, 256)


def _ffn_kernel(x_ref, xc_ref, shift_ref, scale_ref, gate_ref, shift_c_ref, scale_c_ref, gate_c_ref, g_ref,
                wgu_hbm, wd_hbm, o_ref, oc_ref, wgu_ref, wd_ref, *, half):
    _on_first_step(lambda: _load_ffn_weights(wgu_hbm, wd_hbm, half, wgu_ref, wd_ref))
    chains = _row_chains(x_ref.shape[0])
    hs = [_ada_norm(x_ref[rows, :], g_ref, scale_ref, shift_ref) for rows in chains]
    hs[-1] = jnp.concatenate([hs[-1], _ada_norm(xc_ref[...], g_ref, scale_c_ref, shift_c_ref)], axis=0)
    ys = _swiglu(hs, wgu_ref, wd_ref)
    for rows, y in zip(chains, ys):
        o_ref[rows, :] = x_ref[rows, :] + (0.5 * gate_ref[...]) * y[:CHAIN_ROWS]
    oc_ref[...] = xc_ref[...] + (0.5 * gate_c_ref[...]) * ys[-1][CHAIN_ROWS:]


def _ffn_weight_scratch():
    return [pltpu.VMEM((D_MODEL, 2 * D_FF), BF16), pltpu.VMEM((D_FF, D_MODEL), BF16)]


def _ctx_rows_per_step(n_ctx_tok, n_steps):
    assert n_ctx_tok % n_steps == 0 and (n_ctx_tok // n_steps) % 16 == 0
    return n_ctx_tok // n_steps


def _ffn(x, xc, mod3, row_of_token, ctx_row, g3, g_idx, chunk0, w_ffn_gu, w_ffn_down, half):
    n_tok = x.shape[0]
    tile = FFN_TILE
    assert n_tok % tile == 0
    n_steps = n_tok // tile
    c_rows = _ctx_rows_per_step(xc.shape[0], n_steps)
    ctx_mod = lambda chunk: pl.BlockSpec((None, 1, D_MODEL), lambda i: (ctx_row, 0, chunk))
    return pl.pallas_call(
        functools.partial(_ffn_kernel, half=half),
        grid=(n_steps,),
        in_specs=[
            _row_spec(D_MODEL, tile),
            _row_spec(D_MODEL, c_rows),
            _mod_spec(row_of_token, tile, chunk0),
            _mod_spec(row_of_token, tile, chunk0 + 1),
            _mod_spec(row_of_token, tile, chunk0 + 2),
            ctx_mod(chunk0), ctx_mod(chunk0 + 1), ctx_mod(chunk0 + 2),
            _vec_spec(g_idx),
            _HBM, _HBM,
        ],
        out_specs=[_row_spec(D_MODEL, tile), _row_spec(D_MODEL, c_rows)],
        out_shape=[jax.ShapeDtypeStruct((n_tok, D_MODEL), F32),
                   jax.ShapeDtypeStruct(xc.shape, F32)],
        scratch_shapes=_ffn_weight_scratch(),
        compiler_params=_SEQUENTIAL,
        name="ffn",
    )(x, xc, mod3, mod3, mod3, mod3, mod3, mod3, g3, w_ffn_gu, w_ffn_down)


def _rope(z, cos, sin_lo, sin_hi):
    return z * cos + pltpu.roll(z, V_DIM - 16, 1) * sin_lo + pltpu.roll(z, 16, 1) * sin_hi


def _build_rope_tables(freq_ref, cos_ref, slo_ref, shi_ref):
    n_rows = cos_ref.shape[0] // GRID_W
    lane = lax.broadcasted_iota(jnp.int32, (GRID_W, V_DIM), 1)
    on_row_axis = (lane & (HEAD_DIM - 1)) < ROPE_AXIS_DIM
    low_half = (lane & (ROPE_AXIS_DIM - 1)) < ROPE_AXIS_DIM // 2
    freq = freq_ref[...]
    col_ang = lax.broadcasted_iota(jnp.int32, (GRID_W, V_DIM), 0).astype(F32) * freq
    row_ang = lax.broadcasted_iota(jnp.int32, (n_rows, V_DIM), 0).astype(F32) * freq
    cos_col, sin_col = jnp.cos(col_ang), jnp.sin(col_ang)
    cos_row, sin_row = jnp.cos(row_ang), jnp.sin(row_ang)
    zero = jnp.zeros((GRID_W, V_DIM), F32)
    for r in range(n_rows):
        tokens = slice(r * GRID_W, (r + 1) * GRID_W)
        sin = jnp.where(on_row_axis, sin_row[r:r + 1, :], sin_col)
        cos_ref[tokens, :] = jnp.where(on_row_axis, cos_row[r:r + 1, :], cos_col)
        slo_ref[tokens, :] = jnp.where(low_half, -sin, zero)
        shi_ref[tokens, :] = jnp.where(low_half, zero, sin)


def _mix_in_kernel(x_ref, xc_ref, shift_ref, scale_ref, shift_c_ref, scale_c_ref, g_ref, w_hbm, freq_ref,
                   q_ref, k_ref, v_ref, u_ref, sg_ref, kc_ref, vc_ref, w_ref, cos_ref, slo_ref, shi_ref,
                   *, tiles_per_seq):
    def init():
        _load_bf16(w_hbm.at[0], w_ref, 128)
        _build_rope_tables(freq_ref, cos_ref, slo_ref, shi_ref)

    _on_first_step(init)
    tile = x_ref.shape[0]
    chains = _row_chains(tile)
    hs = [_ada_norm(x_ref[rows, :], g_ref, scale_ref, shift_ref) for rows in chains]
    with_ctx = list(hs)
    with_ctx[-1] = jnp.concatenate([hs[-1], _ada_norm(xc_ref[...], g_ref, scale_c_ref, shift_c_ref)], axis=0)
    q_scale = HEAD_DIM ** -0.5 * LOG2E
    pos0 = (pl.program_id(0) % tiles_per_seq) * tile

    def rope_store(z, rows, out_ref, scale):
        tokens = pl.ds(pl.multiple_of(pos0 + rows.start, CHAIN_ROWS), CHAIN_ROWS)
        cos, slo, shi = cos_ref[tokens, :], slo_ref[tokens, :], shi_ref[tokens, :]
        for hd in range(N_HEADS):
            sl = slice(hd * V_DIM, (hd + 1) * V_DIM)
            r = _rope(z[:CHAIN_ROWS, sl], cos, slo, shi)
            out_ref[hd, rows, :] = (r if scale is None else r * scale).astype(BF16)

    def heads_store(z, out_ref, rows):
        for hd in range(N_HEADS):
            out_ref[hd, rows, :] = z[:, hd * V_DIM:(hd + 1) * V_DIM]

    for rows, h in zip(chains, hs):
        rope_store(_mm(h, w_ref[:, Q_OFF:K_OFF]), rows, q_ref, q_scale)
    for rows, h in zip(chains, with_ctx):
        k = _mm(h, w_ref[:, K_OFF:V_OFF])
        rope_store(k, rows, k_ref, None)
        if h.shape[0] > CHAIN_ROWS:
            heads_store(k[CHAIN_ROWS:].astype(BF16), kc_ref, slice(None))
    for rows, h in zip(chains, with_ctx):
        v = _mm(h, w_ref[:, V_OFF:P_OFF]).astype(BF16)
        heads_store(v[:CHAIN_ROWS], v_ref, rows)
        if h.shape[0] > CHAIN_ROWS:
            heads_store(v[CHAIN_ROWS:], vc_ref, slice(None))
    for rows, h in zip(chains, hs):
        u_ref[rows, :] = _mm(h, w_ref[:, P_OFF:G_OFF])
    for rows, h in zip(chains, hs):
        sg_ref[rows, :] = _sigmoid(_mm(h, w_ref[:, G_OFF:IN_COLS])).astype(BF16)


def _mix_in(x, xc, mod3, row_of_token, ctx_row, g3, w_in, seq_len, ctx_len):
    n_tok = x.shape[0]
    tile = MIX_TILE
    n_steps = n_tok // tile
    tiles_per_seq = seq_len // tile
    c_rows = _ctx_rows_per_step(xc.shape[0], n_steps)
    assert ctx_len % c_rows == 0
    c_blocks = ctx_len // c_rows
    half = ROPE_AXIS_DIM // 2
    freqs = ROPE_BASE ** (-jnp.arange(half, dtype=F32) / half)
    freq_lanes = jnp.tile(freqs, V_DIM // half).reshape(1, V_DIM)
    table = pltpu.VMEM((seq_len, V_DIM), F32)
    row = lambda w: _row_spec(w, tile)
    ctx_mod = lambda chunk: pl.BlockSpec((None, 1, D_MODEL), lambda i: (ctx_row, 0, chunk))
    heads = pl.BlockSpec((None, N_HEADS, tile, V_DIM),
                         lambda i: (i // tiles_per_seq, 0, i % tiles_per_seq, 0))
    heads_shape = jax.ShapeDtypeStruct((n_tok // seq_len, N_HEADS, seq_len, V_DIM), BF16)
    ctx_heads = pl.BlockSpec((None, N_HEADS, c_rows, V_DIM), lambda i: (i // c_blocks, 0, i % c_blocks, 0))
    ctx_heads_shape = jax.ShapeDtypeStruct((xc.shape[0] // ctx_len, N_HEADS, ctx_len, V_DIM), BF16)
    return pl.pallas_call(
        functools.partial(_mix_in_kernel, tiles_per_seq=tiles_per_seq),
        grid=(n_steps,),
        in_specs=[
            row(D_MODEL),
            _row_spec(D_MODEL, c_rows),
            _mod_spec(row_of_token, tile, 3),
            _mod_spec(row_of_token, tile, 4),
            ctx_mod(3), ctx_mod(4),
            _vec_spec(1),
            _HBM,
            pl.BlockSpec((1, V_DIM), lambda i: (0, 0)),
        ],
        out_specs=[heads, heads, heads, row(POOL_WIDTH), row(2 * D_MODEL), ctx_heads, ctx_heads],
        out_shape=[heads_shape, heads_shape, heads_shape,
                   jax.ShapeDtypeStruct((n_tok, POOL_WIDTH), F32),
                   jax.ShapeDtypeStruct((n_tok, 2 * D_MODEL), BF16), ctx_heads_shape, ctx_heads_shape],
        scratch_shapes=[pltpu.VMEM((D_MODEL, IN_COLS), BF16), table, table, table],
        compiler_params=_SEQUENTIAL,
        name="mix_in",
    )(x, xc, mod3, mod3, mod3, mod3, g3, w_in, freq_lanes)


def _attn_kernel(lq1_ref, lk1_ref, lq2_ref, lk2_ref, gs_ref, q_ref, kc_ref, kl_ref, vc_ref, vl_ref,
                 o_ref, k_scr, v_scr, sa_ref, sb_ref, ma_ref, mb_ref, oa_ref, ob_ref):
    lam = (jnp.exp(jnp.sum(lq1_ref[...] * lk1_ref[...], axis=-1, keepdims=True))
           - jnp.exp(jnp.sum(lq2_ref[...] * lk2_ref[...], axis=-1, keepdims=True))
           + LAM_INIT)
    gs = gs_ref[...] * (1.0 - LAM_INIT)
    n_heads, seq_len, _ = q_ref.shape
    ctx_len = kc_ref.shape[1]
    n_keys = k_scr.shape[1]
    for hd in range(n_heads):
        k_scr[hd, :ctx_len, :] = kc_ref[hd]
        k_scr[hd, ctx_len:, :] = kl_ref[hd]
        v_scr[hd, :ctx_len, :V_DIM] = vc_ref[hd]
        v_scr[hd, ctx_len:, :V_DIM] = vl_ref[hd]
        v_scr[hd, :, V_DIM:] = jnp.ones((n_keys, V_DIM), BF16)
    first_comp = lax.broadcasted_iota(jnp.int32, (Q_TILE, V_DIM), 1) < HEAD_DIM
    nt = (((1,), (1,)), ((), ()))
    blocks_per_head = seq_len // Q_TILE
    assert blocks_per_head & (blocks_per_head - 1) == 0
    shift = blocks_per_head.bit_length() - 1

    def head_rows(t):
        if isinstance(t, int):
            return t // blocks_per_head, pl.ds((t % blocks_per_head) * Q_TILE, Q_TILE)
        hd = lax.shift_right_logical(t, shift)
        return hd, pl.ds(pl.multiple_of((t - (hd << shift)) * Q_TILE, Q_TILE), Q_TILE)

    def scores(t, s_ref, m_ref):
        hd, rows = head_rows(t)
        q = q_ref[hd, rows, :]
        zero = jnp.zeros_like(q)
        qq = jnp.concatenate([jnp.where(first_comp, q, zero), jnp.where(first_comp, zero, q)], axis=0)
        s = lax.dot_general(qq, k_scr[hd], nt, preferred_element_type=F32)
        s_ref[...] = s
        m_ref[...] = jnp.max(s, axis=-1, keepdims=True)

    def values(t, s_ref, m_ref, ov_ref):
        hd, _ = head_rows(t)
        for comp in range(2):
            r = slice(comp * Q_TILE, (comp + 1) * Q_TILE)
            p = jnp.exp2(s_ref[r, :] - m_ref[r, :]).astype(BF16)
            ov_ref[r, :] = _mm(p, v_scr[hd])

    def finish(t, ov_ref):
        hd, rows = head_rows(t)
        on = ov_ref[:, :V_DIM] / ov_ref[:, V_DIM:]
        o = on[:Q_TILE] - lam * on[Q_TILE:]
        o_ref[hd, rows, :] = _rms_norm(o, gs).astype(BF16)

    n_blocks = n_heads * blocks_per_head
    assert n_blocks % 2 == 0 and n_blocks >= 4
    scores(0, sa_ref, ma_ref)
    scores(1, sb_ref, mb_ref)
    values(0, sa_ref, ma_ref, oa_ref)

    def pair(j, carry):
        scores(2 * j + 2, sa_ref, ma_ref)
        values(2 * j + 1, sb_ref, mb_ref, ob_ref)
        finish(2 * j, oa_ref)
        scores(2 * j + 3, sb_ref, mb_ref)
        values(2 * j + 2, sa_ref, ma_ref, oa_ref)
        finish(2 * j + 1, ob_ref)
        return carry

    lax.fori_loop(0, n_blocks // 2 - 1, pair, 0)
    values(n_blocks - 1, sb_ref, mb_ref, ob_ref)
    finish(n_blocks - 2, oa_ref)
    finish(n_blocks - 1, ob_ref)


def _attention(lams, gs, q, kc, kl, vc, vl):
    bsz, _, seq_len, _ = q.shape
    ctx_len = kc.shape[2]
    n_keys = ctx_len + seq_len
    hg = ATTN_HEADS_PER_STEP
    lam_spec = pl.BlockSpec((1, HEAD_DIM), lambda b, g: (0, 0))
    lat_spec = pl.BlockSpec((None, hg, seq_len, V_DIM), lambda b, g: (b, g, 0, 0))
    ctx_spec = pl.BlockSpec((None, hg, ctx_len, V_DIM), lambda b, g: (b, g, 0, 0))
    return pl.pallas_call(
        _attn_kernel,
        grid=(bsz, N_HEADS // hg),
        in_specs=[lam_spec, lam_spec, lam_spec, lam_spec,
                  pl.BlockSpec((1, V_DIM), lambda b, g: (0, 0)),
                  lat_spec, ctx_spec, lat_spec, ctx_spec, lat_spec],
        out_specs=lat_spec,
        out_shape=jax.ShapeDtypeStruct((bsz, N_HEADS, seq_len, V_DIM), BF16),
        scratch_shapes=[pltpu.VMEM((hg, n_keys, V_DIM), BF16),
                        pltpu.VMEM((hg, n_keys, 2 * V_DIM), BF16),
                        pltpu.VMEM((2 * Q_TILE, n_keys), F32), pltpu.VMEM((2 * Q_TILE, n_keys), F32),
                        pltpu.VMEM((2 * Q_TILE, 1), F32), pltpu.VMEM((2 * Q_TILE, 1), F32),
                        pltpu.VMEM((2 * Q_TILE, 2 * V_DIM), F32), pltpu.VMEM((2 * Q_TILE, 2 * V_DIM), F32)],
        compiler_params=pltpu.CompilerParams(
            dimension_semantics=("parallel", "parallel"), vmem_limit_bytes=VMEM_LIMIT_BYTES),
        name="attn",
    )(*lams, gs, q, kc, kl, vc, vl)


def _pool_branch(u, u_prev, u_next, wpool_ref, pscale_ref, pos0, seq_len):
    u_prev = jnp.where(pos0 > 0, u_prev, 0.0)
    u_next = jnp.where(pos0 + TOKEN_TILE < seq_len, u_next, 0.0)
    ext = jnp.concatenate([u_prev, u, u_next], axis=0)
    n_ext = ext.shape[0]
    edge = lax.broadcasted_iota(jnp.int32, (POOL_HALO, POOL_GROUP_DIM), 0)
    edge_pos = (pos0 + edge, pos0 + (TOKEN_TILE - POOL_HALO) + edge)
    outs = []
    for g, w in enumerate(POOL_WINDOWS):
        cols = slice(g * POOL_GROUP_DIM, (g + 1) * POOL_GROUP_DIM)
        fwd = ext[:, cols]
        span = 1
        while 2 * span < w:
            fwd = fwd + pltpu.roll(fwd, n_ext - span, 0)
            span *= 2
        win = (fwd + pltpu.roll(fwd, span, 0))[POOL_HALO:POOL_HALO + TOKEN_TILE]
        clipped = [win[rows] / (jnp.minimum(pos + w // 2, seq_len) - jnp.maximum(pos - w // 2, 0)).astype(F32)
                   for rows, pos in zip((slice(0, POOL_HALO), slice(TOKEN_TILE - POOL_HALO, TOKEN_TILE)),
                                        edge_pos)]
        mean = jnp.concatenate(
            [clipped[0], win[POOL_HALO:TOKEN_TILE - POOL_HALO] * (1.0 / w), clipped[1]], axis=0)
        pooled = (mean - u[:, cols]).astype(BF16)
        outs.append(_mm(pooled, wpool_ref[cols, :]))
    return jnp.concatenate(outs, axis=-1) * pscale_ref[...]


def _merge_kernel(x_ref, attn_ref, u_ref, up_ref, un_ref, sg_ref,
                  gate_mix_ref, shift_ref, scale_ref, gate_ffn_ref, g_ref, gfin_ref, pscale_ref,
                  wpool_hbm, wba_hbm, wbp_hbm, wout_hbm, wgu_hbm, wd_hbm,
                  o_ref, wpool_ref, wba_ref, wbp_ref, wout_ref, wgu_ref, wd_ref, *, tiles_per_seq, seq_len):
    def load_weights():
        _load_bf16(wpool_hbm.at[0], wpool_ref, 256)
        _load_bf16(wba_hbm.at[0], wba_ref, 256)
        _load_bf16(wbp_hbm.at[0], wbp_ref, 256)
        _load_bf16(wout_hbm.at[0], wout_ref, 256)
        _load_ffn_weights(wgu_hbm, wd_hbm, 1, wgu_ref, wd_ref)

    _on_first_step(load_weights)
    halves = (slice(0, TOKEN_TILE // 2), slice(TOKEN_TILE // 2, TOKEN_TILE))
    pos0 = (pl.program_id(0) % tiles_per_seq) * TOKEN_TILE
    pool = _pool_branch(u_ref[...], up_ref[...], un_ref[...], wpool_ref, pscale_ref, pos0, seq_len)
    ys = [sg_ref[rows, :D_MODEL].astype(F32)
          * _mm(jnp.concatenate([attn_ref[hd, rows, :] for hd in range(N_HEADS)], axis=-1), wba_ref[...])
          for rows in halves]
    ys = [y + sg_ref[rows, D_MODEL:].astype(F32) * _mm(pool[rows].astype(BF16), wbp_ref[...])
          for rows, y in zip(halves, ys)]
    xs = [x_ref[rows, :] + gate_mix_ref[...] * _mm(y.astype(BF16), wout_ref[...])
          for rows, y in zip(halves, ys)]
    x = jnp.concatenate(xs, axis=0)
    h = jnp.concatenate([_ada_norm(xh, g_ref, scale_ref, shift_ref) for xh in xs], axis=0)
    x = x + (0.5 * gate_ffn_ref[...]) * _swiglu([h], wgu_ref, wd_ref)[0]
    o_ref[...] = _rms_norm(x, gfin_ref[...])


def _merge(x, attn, u, sg, mod3, row_of_token, g3, g_final, pool_scale, w_pool, w_ba, w_bp, w_out,
           w_ffn_gu, w_ffn_down, seq_len):
    n_tok = x.shape[0]
    tiles_per_seq = seq_len // TOKEN_TILE
    halo_blocks = TOKEN_TILE // POOL_HALO
    n_halo = n_tok // POOL_HALO
    prev_spec = pl.BlockSpec((POOL_HALO, POOL_WIDTH),
                             lambda i: (jnp.maximum(i * halo_blocks - 1, 0), 0))
    next_spec = pl.BlockSpec((POOL_HALO, POOL_WIDTH),
                             lambda i: (jnp.minimum((i + 1) * halo_blocks, n_halo - 1), 0))
    kern = functools.partial(_merge_kernel, tiles_per_seq=tiles_per_seq, seq_len=seq_len)
    return pl.pallas_call(
        kern,
        grid=(n_tok // TOKEN_TILE,),
        in_specs=[
            _row_spec(D_MODEL),
            pl.BlockSpec((None, N_HEADS, TOKEN_TILE, V_DIM),
                         lambda i: (i // tiles_per_seq, 0, i % tiles_per_seq, 0)),
            _row_spec(POOL_WIDTH), prev_spec, next_spec,
            _row_spec(2 * D_MODEL),
            _mod_spec(row_of_token, TOKEN_TILE, 5), _mod_spec(row_of_token, TOKEN_TILE, 6),
            _mod_spec(row_of_token, TOKEN_TILE, 7), _mod_spec(row_of_token, TOKEN_TILE, 8), _vec_spec(2),
            pl.BlockSpec((1, D_MODEL), lambda i: (0, 0)),
            pl.BlockSpec((1, POOL_WIDTH), lambda i: (0, 0)),
            _HBM, _HBM, _HBM, _HBM, _HBM, _HBM,
        ],
        out_specs=_row_spec(D_MODEL),
        out_shape=jax.ShapeDtypeStruct((n_tok, D_MODEL), F32),
        scratch_shapes=[pltpu.VMEM((POOL_WIDTH, POOL_GROUP_DIM), BF16),
                        pltpu.VMEM((ATTN_WIDTH, D_MODEL), BF16), pltpu.VMEM((POOL_WIDTH, D_MODEL), BF16),
                        pltpu.VMEM((D_MODEL, D_MODEL), BF16)] + _ffn_weight_scratch(),
        compiler_params=_SEQUENTIAL,
        name="merge",
    )(x, attn, u, u, u, sg, mod3, mod3, mod3, mod3, g3, g_final, pool_scale,
      w_pool, w_ba, w_bp, w_out, w_ffn_gu, w_ffn_down)


def kernel(x, c, ctx, c_ctx, w_mod, b_mod, g_norm, w_ffn_gu, w_ffn_down, w_in, lambda_q1, lambda_k1,
           lambda_q2, lambda_k2, g_subln, w_pool, pool_scale, w_branch_attn, w_branch_pool, w_out,
           g_final):
    bsz, seq_len, _ = x.shape
    ctx_len = ctx.shape[1]
    assert w_mod.shape[0] == 1, "single-layer block"
    assert seq_len % max(TOKEN_TILE, FFN_TILE, MIX_TILE) == 0 and bsz + 1 <= MOD_ROWS

    cc = jnp.concatenate([c, c_ctx[None, :], jnp.zeros((MOD_ROWS - bsz - 1, D_MODEL), F32)], axis=0)
    mod3 = _modulation(cc, w_mod[0], b_mod)
    g3 = g_norm[0].reshape(3, 1, D_MODEL)

    lat_row = lambda first_token: first_token // seq_len
    ctx_row = bsz

    lat, cx = _ffn(x.reshape(bsz * seq_len, D_MODEL), ctx.reshape(bsz * ctx_len, D_MODEL), mod3,
                   lat_row, ctx_row, g3, 0, 0, w_ffn_gu, w_ffn_down, 0)
    q, k_l, v_l, u, sg, k_c, v_c = _mix_in(lat, cx, mod3, lat_row, ctx_row, g3, w_in, seq_len, ctx_len)

    lams = [v.reshape(1, HEAD_DIM) for v in (lambda_q1, lambda_k1, lambda_q2, lambda_k2)]
    attn = _attention(lams, g_subln.reshape(1, V_DIM), q, k_c, k_l, v_c, v_l)

    out = _merge(lat, attn, u, sg, mod3, lat_row, g3,
                 g_final.reshape(1, D_MODEL), pool_scale.reshape(1, POOL_WIDTH),
                 w_pool.reshape(1, POOL_WIDTH, POOL_GROUP_DIM), w_branch_attn, w_branch_pool, w_out,
                 w_ffn_gu, w_ffn_down, seq_len)
    return out.reshape(bsz, seq_len, D_MODEL)
```

```python
import functools
import math

import jax
import jax.numpy as jnp
from jax import lax
from jax.experimental import pallas as pl
from jax.experimental.pallas import tpu as pltpu

F32 = jnp.float32
BF16 = jnp.bfloat16

D_MODEL = 1024
N_HEADS = 8
HEAD_DIM = 64
V_DIM = 2 * HEAD_DIM
QK_WIDTH = N_HEADS * 2 * HEAD_DIM
ATTN_WIDTH = N_HEADS * V_DIM
POOL_WINDOWS = (2, 4, 8, 16)
POOL_GROUP_DIM = 128
POOL_WIDTH = len(POOL_WINDOWS) * POOL_GROUP_DIM
POOL_HALO = 8
D_FF = 2816
GRID_W = 64
ROPE_BASE = 10000.0
ROPE_AXIS_DIM = HEAD_DIM // 2
EPS = 1e-6
LAM_INIT = 0.8 - 0.6 * math.exp(-0.3 * 0)

Q_OFF = 0
K_OFF = Q_OFF + QK_WIDTH
V_OFF = K_OFF + QK_WIDTH
P_OFF = V_OFF + ATTN_WIDTH
G_OFF = P_OFF + POOL_WIDTH
IN_COLS = G_OFF + 2 * D_MODEL

LOG2E = 1.4426950408889634

VMEM_LIMIT_BYTES = 60 * 1024 * 1024
TOKEN_TILE = 512
FFN_TILE = 1024
MIX_TILE = 1024
CHAIN_ROWS = 512
FF_CHUNK = 256
Q_TILE = 512
ATTN_HEADS_PER_STEP = 4
MOD_ROWS = 24
MOD_BLOCK = 2304


def _sigmoid(x):
    return 0.5 * jnp.tanh(0.5 * x) + 0.5


def _rms_norm(x, g):
    return x * lax.rsqrt(jnp.mean(x * x, axis=-1, keepdims=True) + EPS) * g


def _ada_norm(x, g_ref, scale_ref, shift_ref):
    w = g_ref[...] * (1.0 + scale_ref[...])
    inv = lax.rsqrt(jnp.mean(x * x, axis=-1, keepdims=True) + EPS)
    return (x * inv * w + shift_ref[...]).astype(BF16)


def _mm(a, b):
    return jnp.dot(a, b, preferred_element_type=F32)


def _row_chains(tile):
    return [slice(r, r + CHAIN_ROWS) for r in range(0, tile, CHAIN_ROWS)]


def _swiglu_chunks(hs, accs, chunk_ids, wgu_ref, wd_ref):
    accs = list(accs)
    chunk_ids = list(chunk_ids)
    order = [(chunk_ids[s - c], c) for s in range(len(chunk_ids) + len(hs) - 1)
             for c in range(len(hs)) if 0 <= s - c < len(chunk_ids)]
    for j, c in order:
        lo = j * FF_CHUNK
        a = _mm(hs[c], wgu_ref[:, lo:lo + FF_CHUNK])
        b = _mm(hs[c], wgu_ref[:, D_FF + lo:D_FF + lo + FF_CHUNK])
        t = (a * _sigmoid(a) * b).astype(BF16)
        part = _mm(t, wd_ref[lo:lo + FF_CHUNK, :])
        accs[c] = part if accs[c] is None else accs[c] + part
    return accs


def _swiglu(hs, wgu_ref, wd_ref):
    return _swiglu_chunks(hs, [None] * len(hs), range(D_FF // FF_CHUNK), wgu_ref, wd_ref)


def _load_bf16(src, dst, chunk_rows):
    n_rows, n_cols = dst.shape
    assert src.shape == dst.shape and n_rows % chunk_rows == 0
    n_chunks = n_rows // chunk_rows

    def body(stage, sem):
        def copy(c):
            return pltpu.make_async_copy(src.at[pl.ds(c * chunk_rows, chunk_rows), :],
                                         stage.at[c % 2], sem.at[c % 2])
        copy(0).start(priority=0)
        for c in range(n_chunks):
            if c + 1 < n_chunks:
                copy(c + 1).start(priority=(c + 1) % 2)
            copy(c).wait()
            dst[pl.ds(c * chunk_rows, chunk_rows), :] = stage[c % 2].astype(BF16)

    pl.run_scoped(body, pltpu.VMEM((2, chunk_rows, n_cols), F32), pltpu.SemaphoreType.DMA((2,)))


def _on_first_step(fn):
    pl.when(pl.program_id(0) == 0)(fn)


_HBM = pl.BlockSpec(memory_space=pl.ANY)
_SEQUENTIAL = pltpu.CompilerParams(dimension_semantics=("arbitrary",), vmem_limit_bytes=VMEM_LIMIT_BYTES)


def _mod_kernel(c_ref, w_ref, b_ref, o_ref):
    c = c_ref[...]
    s = (c * _sigmoid(c)).astype(BF16)
    o_ref[:, 0, :] = _mm(s, w_ref[...].astype(BF16)) + b_ref[...]


def _modulation(cc, w_mod, b_mod):
    n_out = w_mod.shape[1]
    blk = MOD_BLOCK
    assert n_out % blk == 0
    return pl.pallas_call(
        _mod_kernel,
        grid=(n_out // blk,),
        in_specs=[
            pl.BlockSpec((MOD_ROWS, D_MODEL), lambda j: (0, 0)),
            pl.BlockSpec((D_MODEL, blk), lambda j: (0, j)),
            pl.BlockSpec((1, blk), lambda j: (0, j)),
        ],
        out_specs=pl.BlockSpec((MOD_ROWS, 1, blk), lambda j: (0, 0, j)),
        out_shape=jax.ShapeDtypeStruct((MOD_ROWS, 1, n_out), F32),
        compiler_params=pltpu.CompilerParams(
            dimension_semantics=("arbitrary",), vmem_limit_bytes=VMEM_LIMIT_BYTES),
        name="mod",
    )(cc, w_mod, b_mod)


def _mod_spec(row_of_token, tile, chunk):
    return pl.BlockSpec((None, 1, D_MODEL), lambda i: (row_of_token(i * tile), 0, chunk))


def _row_spec(width, tile=TOKEN_TILE):
    return pl.BlockSpec((tile, width), lambda i: (i, 0))


def _vec_spec(idx):
    return pl.BlockSpec((None, 1, D_MODEL), lambda i: (idx, 0, 0))


def _load_ffn_weights(wgu_hbm, wd_hbm, half, wgu_ref, wd_ref):
    _load_bf16(wgu_hbm.at[0, half], wgu_ref, 128)
    _load_bf16(wd_hbm.at[0, half], wd_ref, 256)


def _ffn_kernel(x_ref, xc_ref, shift_ref, scale_ref, gate_ref, shift_c_ref, scale_c_ref, gate_c_ref, g_ref,
                wgu_hbm, wd_hbm, o_ref, oc_ref, wgu_ref, wd_ref, *, half):
    _on_first_step(lambda: _load_ffn_weights(wgu_hbm, wd_hbm, half, wgu_ref, wd_ref))
    chains = _row_chains(x_ref.shape[0])
    hs = [_ada_norm(x_ref[rows, :], g_ref, scale_ref, shift_ref) for rows in chains]
    hs[-1] = jnp.concatenate([hs[-1], _ada_norm(xc_ref[...], g_ref, scale_c_ref, shift_c_ref)], axis=0)
    ys = _swiglu(hs, wgu_ref, wd_ref)
    for rows, y in zip(chains, ys):
        o_ref[rows, :] = x_ref[rows, :] + (0.5 * gate_ref[...]) * y[:CHAIN_ROWS]
    oc_ref[...] = xc_ref[...] + (0.5 * gate_c_ref[...]) * ys[-1][CHAIN_ROWS:]


def _ffn_weight_scratch():
    return [pltpu.VMEM((D_MODEL, 2 * D_FF), BF16), pltpu.VMEM((D_FF, D_MODEL), BF16)]


def _ctx_rows_per_step(n_ctx_tok, n_steps):
    assert n_ctx_tok % n_steps == 0 and (n_ctx_tok // n_steps) % 16 == 0
    return n_ctx_tok // n_steps


def _ffn(x, xc, mod3, row_of_token, ctx_row, g3, g_idx, chunk0, w_ffn_gu, w_ffn_down, half):
    n_tok = x.shape[0]
    tile = FFN_TILE
    assert n_tok % tile == 0
    n_steps = n_tok // tile
    c_rows = _ctx_rows_per_step(xc.shape[0], n_steps)
    ctx_mod = lambda chunk: pl.BlockSpec((None, 1, D_MODEL), lambda i: (ctx_row, 0, chunk))
    return pl.pallas_call(
        functools.partial(_ffn_kernel, half=half),
        grid=(n_steps,),
        in_specs=[
            _row_spec(D_MODEL, tile),
            _row_spec(D_MODEL, c_rows),
            _mod_spec(row_of_token, tile, chunk0),
            _mod_spec(row_of_token, tile, chunk0 + 1),
            _mod_spec(row_of_token, tile, chunk0 + 2),
            ctx_mod(chunk0), ctx_mod(chunk0 + 1), ctx_mod(chunk0 + 2),
            _vec_spec(g_idx),
            _HBM, _HBM,
        ],
        out_specs=[_row_spec(D_MODEL, tile), _row_spec(D_MODEL, c_rows)],
        out_shape=[jax.ShapeDtypeStruct((n_tok, D_MODEL), F32),
                   jax.ShapeDtypeStruct(xc.shape, F32)],
        scratch_shapes=_ffn_weight_scratch(),
        compiler_params=_SEQUENTIAL,
        name="ffn",
    )(x, xc, mod3, mod3, mod3, mod3, mod3, mod3, g3, w_ffn_gu, w_ffn_down)


def _rope(z, cos, sin_lo, sin_hi):
    return z * cos + pltpu.roll(z, V_DIM - 16, 1) * sin_lo + pltpu.roll(z, 16, 1) * sin_hi


def _build_rope_tables(freq_ref, cos_ref, slo_ref, shi_ref):
    n_rows = cos_ref.shape[0] // GRID_W
    lane = lax.broadcasted_iota(jnp.int32, (GRID_W, V_DIM), 1)
    on_row_axis = (lane & (HEAD_DIM - 1)) < ROPE_AXIS_DIM
    low_half = (lane & (ROPE_AXIS_DIM - 1)) < ROPE_AXIS_DIM // 2
    freq = freq_ref[...]
    col_ang = lax.broadcasted_iota(jnp.int32, (GRID_W, V_DIM), 0).astype(F32) * freq
    row_ang = lax.broadcasted_iota(jnp.int32, (n_rows, V_DIM), 0).astype(F32) * freq
    cos_col, sin_col = jnp.cos(col_ang), jnp.sin(col_ang)
    cos_row, sin_row = jnp.cos(row_ang), jnp.sin(row_ang)
    zero = jnp.zeros((GRID_W, V_DIM), F32)
    for r in range(n_rows):
        tokens = slice(r * GRID_W, (r + 1) * GRID_W)
        sin = jnp.where(on_row_axis, sin_row[r:r + 1, :], sin_col)
        cos_ref[tokens, :] = jnp.where(on_row_axis, cos_row[r:r + 1, :], cos_col)
        slo_ref[tokens, :] = jnp.where(low_half, -sin, zero)
        shi_ref[tokens, :] = jnp.where(low_half, zero, sin)


def _mix_in_kernel(x_ref, xc_ref, shift_ref, scale_ref, shift_c_ref, scale_c_ref, g_ref, w_hbm, freq_ref,
                   q_ref, k_ref, v_ref, u_ref, sg_ref, kc_ref, vc_ref, w_ref, cos_ref, slo_ref, shi_ref,
                   *, tiles_per_seq):
    def init():
        _load_bf16(w_hbm.at[0], w_ref, 128)
        _build_rope_tables(freq_ref, cos_ref, slo_ref, shi_ref)

    _on_first_step(init)
    tile = x_ref.shape[0]
    chains = _row_chains(tile)
    hs = [_ada_norm(x_ref[rows, :], g_ref, scale_ref, shift_ref) for rows in chains]
    with_ctx = list(hs)
    with_ctx[-1] = jnp.concatenate([hs[-1], _ada_norm(xc_ref[...], g_ref, scale_c_ref, shift_c_ref)], axis=0)
    q_scale = HEAD_DIM ** -0.5 * LOG2E
    pos0 = (pl.program_id(0) % tiles_per_seq) * tile

    def rope_store(z, rows, out_ref, scale):
        tokens = pl.ds(pl.multiple_of(pos0 + rows.start, CHAIN_ROWS), CHAIN_ROWS)
        cos, slo, shi = cos_ref[tokens, :], slo_ref[tokens, :], shi_ref[tokens, :]
        for hd in range(N_HEADS):
            sl = slice(hd * V_DIM, (hd + 1) * V_DIM)
            r = _rope(z[:CHAIN_ROWS, sl], cos, slo, shi)
            out_ref[hd, rows, :] = (r if scale is None else r * scale).astype(BF16)

    def heads_store(z, out_ref, rows):
        for hd in range(N_HEADS):
            out_ref[hd, rows, :] = z[:, hd * V_DIM:(hd + 1) * V_DIM]

    for rows, h in zip(chains, hs):
        rope_store(_mm(h, w_ref[:, Q_OFF:K_OFF]), rows, q_ref, q_scale)
    for rows, h in zip(chains, with_ctx):
        k = _mm(h, w_ref[:, K_OFF:V_OFF])
        rope_store(k, rows, k_ref, None)
        if h.shape[0] > CHAIN_ROWS:
            heads_store(k[CHAIN_ROWS:].astype(BF16), kc_ref, slice(None))
    for rows, h in zip(chains, with_ctx):
        v = _mm(h, w_ref[:, V_OFF:P_OFF]).astype(BF16)
        heads_store(v[:CHAIN_ROWS], v_ref, rows)
        if h.shape[0] > CHAIN_ROWS:
            heads_store(v[CHAIN_ROWS:], vc_ref, slice(None))
    for rows, h in zip(chains, hs):
        u_ref[rows, :] = _mm(h, w_ref[:, P_OFF:G_OFF])
    for rows, h in zip(chains, hs):
        sg_ref[rows, :] = _sigmoid(_mm(h, w_ref[:, G_OFF:IN_COLS])).astype(BF16)


def _mix_in(x, xc, mod3, row_of_token, ctx_row, g3, w_in, seq_len, ctx_len):
    n_tok = x.shape[0]
    tile = MIX_TILE
    n_steps = n_tok // tile
    tiles_per_seq = seq_len // tile
    c_rows = _ctx_rows_per_step(xc.shape[0], n_steps)
    assert ctx_len % c_rows == 0
    c_blocks = ctx_len // c_rows
    half = ROPE_AXIS_DIM // 2
    freqs = ROPE_BASE ** (-jnp.arange(half, dtype=F32) / half)
    freq_lanes = jnp.tile(freqs, V_DIM // half).reshape(1, V_DIM)
    table = pltpu.VMEM((seq_len, V_DIM), F32)
    row = lambda w: _row_spec(w, tile)
    ctx_mod = lambda chunk: pl.BlockSpec((None, 1, D_MODEL), lambda i: (ctx_row, 0, chunk))
    heads = pl.BlockSpec((None, N_HEADS, tile, V_DIM),
                         lambda i: (i // tiles_per_seq, 0, i % tiles_per_seq, 0))
    heads_shape = jax.ShapeDtypeStruct((n_tok // seq_len, N_HEADS, seq_len, V_DIM), BF16)
    ctx_heads = pl.BlockSpec((None, N_HEADS, c_rows, V_DIM), lambda i: (i // c_blocks, 0, i % c_blocks, 0))
    ctx_heads_shape = jax.ShapeDtypeStruct((xc.shape[0] // ctx_len, N_HEADS, ctx_len, V_DIM), BF16)
    return pl.pallas_call(
        functools.partial(_mix_in_kernel, tiles_per_seq=tiles_per_seq),
        grid=(n_steps,),
        in_specs=[
            row(D_MODEL),
            _row_spec(D_MODEL, c_rows),
            _mod_spec(row_of_token, tile, 3),
            _mod_spec(row_of_token, tile, 4),
            ctx_mod(3), ctx_mod(4),
            _vec_spec(1),
            _HBM,
            pl.BlockSpec((1, V_DIM), lambda i: (0, 0)),
        ],
        out_specs=[heads, heads, heads, row(POOL_WIDTH), row(2 * D_MODEL), ctx_heads, ctx_heads],
        out_shape=[heads_shape, heads_shape, heads_shape,
                   jax.ShapeDtypeStruct((n_tok, POOL_WIDTH), F32),
                   jax.ShapeDtypeStruct((n_tok, 2 * D_MODEL), BF16), ctx_heads_shape, ctx_heads_shape],
        scratch_shapes=[pltpu.VMEM((D_MODEL, IN_COLS), BF16), table, table, table],
        compiler_params=_SEQUENTIAL,
        name="mix_in",
    )(x, xc, mod3, mod3, mod3, mod3, g3, w_in, freq_lanes)


def _attn_kernel(lq1_ref, lk1_ref, lq2_ref, lk2_ref, gs_ref, q_ref, kc_ref, kl_ref, vc_ref, vl_ref,
                 o_ref, k_scr, v_scr, sa_ref, sb_ref, ma_ref, mb_ref, oa_ref, ob_ref):
    lam = (jnp.exp(jnp.sum(lq1_ref[...] * lk1_ref[...], axis=-1, keepdims=True))
           - jnp.exp(jnp.sum(lq2_ref[...] * lk2_ref[...], axis=-1, keepdims=True))
           + LAM_INIT)
    gs = gs_ref[...] * (1.0 - LAM_INIT)
    n_heads, seq_len, _ = q_ref.shape
    ctx_len = kc_ref.shape[1]
    n_keys = k_scr.shape[1]
    for hd in range(n_heads):
        k_scr[hd, :ctx_len, :] = kc_ref[hd]
        k_scr[hd, ctx_len:, :] = kl_ref[hd]
        v_scr[hd, :ctx_len, :V_DIM] = vc_ref[hd]
        v_scr[hd, ctx_len:, :V_DIM] = vl_ref[hd]
        v_scr[hd, :, V_DIM:] = jnp.ones((n_keys, V_DIM), BF16)
    first_comp = lax.broadcasted_iota(jnp.int32, (Q_TILE, V_DIM), 1) < HEAD_DIM
    nt = (((1,), (1,)), ((), ()))
    blocks_per_head = seq_len // Q_TILE
    assert blocks_per_head & (blocks_per_head - 1) == 0
    shift = blocks_per_head.bit_length() - 1

    def head_rows(t):
        if isinstance(t, int):
            return t // blocks_per_head, pl.ds((t % blocks_per_head) * Q_TILE, Q_TILE)
        hd = lax.shift_right_logical(t, shift)
        return hd, pl.ds(pl.multiple_of((t - (hd << shift)) * Q_TILE, Q_TILE), Q_TILE)

    def scores(t, s_ref, m_ref):
        hd, rows = head_rows(t)
        q = q_ref[hd, rows, :]
        zero = jnp.zeros_like(q)
        qq = jnp.concatenate([jnp.where(first_comp, q, zero), jnp.where(first_comp, zero, q)], axis=0)
        s = lax.dot_general(qq, k_scr[hd], nt, preferred_element_type=F32)
        s_ref[...] = s
        m_ref[...] = jnp.max(s, axis=-1, keepdims=True)

    def values(t, s_ref, m_ref, ov_ref):
        hd, _ = head_rows(t)
        for comp in range(2):
            r = slice(comp * Q_TILE, (comp + 1) * Q_TILE)
            p = jnp.exp2(s_ref[r, :] - m_ref[r, :]).astype(BF16)
            ov_ref[r, :] = _mm(p, v_scr[hd])

    def finish(t, ov_ref):
        hd, rows = head_rows(t)
        on = ov_ref[:, :V_DIM] / ov_ref[:, V_DIM:]
        o = on[:Q_TILE] - lam * on[Q_TILE:]
        o_ref[hd, rows, :] = _rms_norm(o, gs).astype(BF16)

    n_blocks = n_heads * blocks_per_head
    assert n_blocks % 2 == 0 and n_blocks >= 4
    scores(0, sa_ref, ma_ref)
    scores(1, sb_ref, mb_ref)
    values(0, sa_ref, ma_ref, oa_ref)

    def pair(j, carry):
        scores(2 * j + 2, sa_ref, ma_ref)
        values(2 * j + 1, sb_ref, mb_ref, ob_ref)
        finish(2 * j, oa_ref)
        scores(2 * j + 3, sb_ref, mb_ref)
        values(2 * j + 2, sa_ref, ma_ref, oa_ref)
        finish(2 * j + 1, ob_ref)
        return carry

    lax.fori_loop(0, n_blocks // 2 - 1, pair, 0)
    values(n_blocks - 1, sb_ref, mb_ref, ob_ref)
    finish(n_blocks - 2, oa_ref)
    finish(n_blocks - 1, ob_ref)


def _attention(lams, gs, q, kc, kl, vc, vl):
    bsz, _, seq_len, _ = q.shape
    ctx_len = kc.shape[2]
    n_keys = ctx_len + seq_len
    hg = ATTN_HEADS_PER_STEP
    lam_spec = pl.BlockSpec((1, HEAD_DIM), lambda b, g: (0, 0))
    lat_spec = pl.BlockSpec((None, hg, seq_len, V_DIM), lambda b, g: (b, g, 0, 0))
    ctx_spec = pl.BlockSpec((None, hg, ctx_len, V_DIM), lambda b, g: (b, g, 0, 0))
    return pl.pallas_call(
        _attn_kernel,
        grid=(bsz, N_HEADS // hg),
        in_specs=[lam_spec, lam_spec, lam_spec, lam_spec,
                  pl.BlockSpec((1, V_DIM), lambda b, g: (0, 0)),
                  lat_spec, ctx_spec, lat_spec, ctx_spec, lat_spec],
        out_specs=lat_spec,
        out_shape=jax.ShapeDtypeStruct((bsz, N_HEADS, seq_len, V_DIM), BF16),
        scratch_shapes=[pltpu.VMEM((hg, n_keys, V_DIM), BF16),
                        pltpu.VMEM((hg, n_keys, 2 * V_DIM), BF16),
                        pltpu.VMEM((2 * Q_TILE, n_keys), F32), pltpu.VMEM((2 * Q_TILE, n_keys), F32),
                        pltpu.VMEM((2 * Q_TILE, 1), F32), pltpu.VMEM((2 * Q_TILE, 1), F32),
                        pltpu.VMEM((2 * Q_TILE, 2 * V_DIM), F32), pltpu.VMEM((2 * Q_TILE, 2 * V_DIM), F32)],
        compiler_params=pltpu.CompilerParams(
            dimension_semantics=("parallel", "parallel"), vmem_limit_bytes=VMEM_LIMIT_BYTES),
        name="attn",
    )(*lams, gs, q, kc, kl, vc, vl)


def _pool_branch(u, u_prev, u_next, wpool_ref, pscale_ref, pos0, seq_len):
    u_prev = jnp.where(pos0 > 0, u_prev, 0.0)
    u_next = jnp.where(pos0 + TOKEN_TILE < seq_len, u_next, 0.0)
    ext = jnp.concatenate([u_prev, u, u_next], axis=0)
    n_ext = ext.shape[0]
    edge = lax.broadcasted_iota(jnp.int32, (POOL_HALO, POOL_GROUP_DIM), 0)
    edge_pos = (pos0 + edge, pos0 + (TOKEN_TILE - POOL_HALO) + edge)
    outs = []
    for g, w in enumerate(POOL_WINDOWS):
        cols = slice(g * POOL_GROUP_DIM, (g + 1) * POOL_GROUP_DIM)
        fwd = ext[:, cols]
        span = 1
        while 2 * span < w:
            fwd = fwd + pltpu.roll(fwd, n_ext - span, 0)
            span *= 2
        win = (fwd + pltpu.roll(fwd, span, 0))[POOL_HALO:POOL_HALO + TOKEN_TILE]
        clipped = [win[rows] / (jnp.minimum(pos + w // 2, seq_len) - jnp.maximum(pos - w // 2, 0)).astype(F32)
                   for rows, pos in zip((slice(0, POOL_HALO), slice(TOKEN_TILE - POOL_HALO, TOKEN_TILE)),
                                        edge_pos)]
        mean = jnp.concatenate(
            [clipped[0], win[POOL_HALO:TOKEN_TILE - POOL_HALO] * (1.0 / w), clipped[1]], axis=0)
        pooled = (mean - u[:, cols]).astype(BF16)
        outs.append(_mm(pooled, wpool_ref[cols, :]))
    return jnp.concatenate(outs, axis=-1) * pscale_ref[...]


def _merge_kernel(x_ref, attn_ref, u_ref, up_ref, un_ref, sg_ref,
                  gate_mix_ref, shift_ref, scale_ref, gate_ffn_ref, g_ref, gfin_ref, pscale_ref,
                  wpool_hbm, wba_hbm, wbp_hbm, wout_hbm, wgu_hbm, wd_hbm,
                  o_ref, wpool_ref, wba_ref, wbp_ref, wout_ref, wgu_ref, wd_ref, *, tiles_per_seq, seq_len):
    def load_weights():
        _load_bf16(wpool_hbm.at[0], wpool_ref, 256)
        _load_bf16(wba_hbm.at[0], wba_ref, 256)
        _load_bf16(wbp_hbm.at[0], wbp_ref, 256)
        _load_bf16(wout_hbm.at[0], wout_ref, 256)
        _load_ffn_weights(wgu_hbm, wd_hbm, 1, wgu_ref, wd_ref)

    _on_first_step(load_weights)
    halves = (slice(0, TOKEN_TILE // 2), slice(TOKEN_TILE // 2, TOKEN_TILE))
    pos0 = (pl.program_id(0) % tiles_per_seq) * TOKEN_TILE
    ys = [sg_ref[rows, :D_MODEL].astype(F32)
          * _mm(jnp.concatenate([attn_ref[hd, rows, :] for hd in range(N_HEADS)], axis=-1), wba_ref[...])
          for rows in halves]
    pool = _pool_branch(u_ref[...], up_ref[...], un_ref[...], wpool_ref, pscale_ref, pos0, seq_len)
    ys = [y + sg_ref[rows, D_MODEL:].astype(F32) * _mm(pool[rows].astype(BF16), wbp_ref[...])
          for rows, y in zip(halves, ys)]
    xs = [x_ref[rows, :] + gate_mix_ref[...] * _mm(y.astype(BF16), wout_ref[...])
          for rows, y in zip(halves, ys)]
    x = jnp.concatenate(xs, axis=0)
    h = jnp.concatenate([_ada_norm(xh, g_ref, scale_ref, shift_ref) for xh in xs], axis=0)
    x = x + (0.5 * gate_ffn_ref[...]) * _swiglu([h], wgu_ref, wd_ref)[0]
    o_ref[...] = _rms_norm(x, gfin_ref[...])


def _merge(x, attn, u, sg, mod3, row_of_token, g3, g_final, pool_scale, w_pool, w_ba, w_bp, w_out,
           w_ffn_gu, w_ffn_down, seq_len):
    n_tok = x.shape[0]
    tiles_per_seq = seq_len // TOKEN_TILE
    halo_blocks = TOKEN_TILE // POOL_HALO
    n_halo = n_tok // POOL_HALO
    prev_spec = pl.BlockSpec((POOL_HALO, POOL_WIDTH),
                             lambda i: (jnp.maximum(i * halo_blocks - 1, 0), 0))
    next_spec = pl.BlockSpec((POOL_HALO, POOL_WIDTH),
                             lambda i: (jnp.minimum((i + 1) * halo_blocks, n_halo - 1), 0))
    kern = functools.partial(_merge_kernel, tiles_per_seq=tiles_per_seq, seq_len=seq_len)
    return pl.pallas_call(
        kern,
        grid=(n_tok // TOKEN_TILE,),
        in_specs=[
            _row_spec(D_MODEL),
            pl.BlockSpec((None, N_HEADS, TOKEN_TILE, V_DIM),
                         lambda i: (i // tiles_per_seq, 0, i % tiles_per_seq, 0)),
            _row_spec(POOL_WIDTH), prev_spec, next_spec,
            _row_spec(2 * D_MODEL),
            _mod_spec(row_of_token, TOKEN_TILE, 5), _mod_spec(row_of_token, TOKEN_TILE, 6),
            _mod_spec(row_of_token, TOKEN_TILE, 7), _mod_spec(row_of_token, TOKEN_TILE, 8), _vec_spec(2),
            pl.BlockSpec((1, D_MODEL), lambda i: (0, 0)),
            pl.BlockSpec((1, POOL_WIDTH), lambda i: (0, 0)),
            _HBM, _HBM, _HBM, _HBM, _HBM, _HBM,
        ],
        out_specs=_row_spec(D_MODEL),
        out_shape=jax.ShapeDtypeStruct((n_tok, D_MODEL), F32),
        scratch_shapes=[pltpu.VMEM((POOL_WIDTH, POOL_GROUP_DIM), BF16),
                        pltpu.VMEM((ATTN_WIDTH, D_MODEL), BF16), pltpu.VMEM((POOL_WIDTH, D_MODEL), BF16),
                        pltpu.VMEM((D_MODEL, D_MODEL), BF16)] + _ffn_weight_scratch(),
        compiler_params=_SEQUENTIAL,
        name="merge",
    )(x, attn, u, u, u, sg, mod3, mod3, mod3, mod3, g3, g_final, pool_scale,
      w_pool, w_ba, w_bp, w_out, w_ffn_gu, w_ffn_down)


def kernel(x, c, ctx, c_ctx, w_mod, b_mod, g_norm, w_ffn_gu, w_ffn_down, w_in, lambda_q1, lambda_k1,
           lambda_q2, lambda_k2, g_subln, w_pool, pool_scale, w_branch_attn, w_branch_pool, w_out,
           g_final):
    bsz, seq_len, _ = x.shape
    ctx_len = ctx.shape[1]
    assert w_mod.shape[0] == 1, "single-layer block"
    assert seq_len % max(TOKEN_TILE, FFN_TILE, MIX_TILE) == 0 and bsz + 1 <= MOD_ROWS

    cc = jnp.concatenate([c, c_ctx[None, :], jnp.zeros((MOD_ROWS - bsz - 1, D_MODEL), F32)], axis=0)
    mod3 = _modulation(cc, w_mod[0], b_mod)
    g3 = g_norm[0].reshape(3, 1, D_MODEL)

    lat_row = lambda first_token: first_token // seq_len
    ctx_row = bsz

    lat, cx = _ffn(x.reshape(bsz * seq_len, D_MODEL), ctx.reshape(bsz * ctx_len, D_MODEL), mod3,
                   lat_row, ctx_row, g3, 0, 0, w_ffn_gu, w_ffn_down, 0)
    q, k_l, v_l, u, sg, k_c, v_c = _mix_in(lat, cx, mod3, lat_row, ctx_row, g3, w_in, seq_len, ctx_len)

    lams = [v.reshape(1, HEAD_DIM) for v in (lambda_q1, lambda_k1, lambda_q2, lambda_k2)]
    attn = _attention(lams, g_subln.reshape(1, V_DIM), q, k_c, k_l, v_c, v_l)

    out = _merge(lat, attn, u, sg, mod3, lat_row, g3,
                 g_final.reshape(1, D_MODEL), pool_scale.reshape(1, POOL_WIDTH),
                 w_pool.reshape(1, POOL_WIDTH, POOL_GROUP_DIM), w_branch_attn, w_branch_pool, w_out,
                 w_ffn_gu, w_ffn_down, seq_len)
    return out.reshape(bsz, seq_len, D_MODEL)
```
